```python
import jax, jax.numpy as jnp
from jax import lax
import numpy as np


D_MODEL = 1024
BATCH = 1
SEQ = 16384
DEPTH = 2
DEC_BATCH = 8
DEC_SEQ = 64
PAST_LEN = 1024

CHUNK = 64
N_A = DEPTH // 2
N_B = DEPTH - N_A
D_CONV = D_MODEL
CONV_WIDTH = 31
N_HEADS = 16
N_KV_HEADS = 2
GROUP = N_HEADS // N_KV_HEADS
HEAD_DIM = 64
ATTN_DIM = N_HEADS * HEAD_DIM
KV_DIM = N_KV_HEADS * HEAD_DIM
WINDOW = 128
WINDOW_CHUNKS = WINDOW // CHUNK
EPS = 1e-6
NEG_INF = -1e30

kernel_name = 'streaming_conformer_swa_sink_yoco'


def rmsnorm(x, g):
    xf = x.astype(jnp.float32)
    y = xf * lax.rsqrt(jnp.mean(xf * xf, axis=-1, keepdims=True) + EPS)
    return (y * g.astype(jnp.float32)).astype(x.dtype)


def layernorm(x, g, b):
    xf = x.astype(jnp.float32)
    mu = jnp.mean(xf, axis=-1, keepdims=True)
    xc = xf - mu
    y = xc * lax.rsqrt(jnp.mean(xc * xc, axis=-1, keepdims=True) + EPS)
    return (y * g.astype(jnp.float32) + b.astype(jnp.float32)).astype(x.dtype)


def alibi_slopes():
    h = jnp.arange(1, N_HEADS + 1, dtype=jnp.float32)
    return (2.0 ** (-8.0 * h / N_HEADS)).reshape(N_KV_HEADS, GROUP)


def conformer_conv_layer(x, hist, pre_g, w_in, b_in, w_dw, b_dw, ln_g, ln_b, w_out, b_out, post_g):
    h = rmsnorm(x, pre_g)
    z = h @ w_in + b_in
    a, gl, gate = jnp.split(z, 3, axis=-1)
    u = a * jax.nn.sigmoid(gl)
    u_hist = jnp.concatenate([hist, u], axis=1)
    c = lax.conv_general_dilated(u_hist, w_dw[:, None, :], (1,), 'VALID',
                                 dimension_numbers=('NWC', 'WIO', 'NWC'),
                                 feature_group_count=D_CONV) + b_dw
    c = jax.nn.silu(layernorm(c, ln_g, ln_b)) * jax.nn.silu(gate)
    y = c @ w_out + b_out
    return x + rmsnorm(y, post_g), u_hist[:, -(CONV_WIDTH - 1):]


def sink_alibi_attention(q, k, v, q_pos, k_pos, sinks):
    s = jnp.einsum('bnqkgd,bnskd->bnkgqs', q.astype(jnp.float32) * (HEAD_DIM ** -0.5),
                   k.astype(jnp.float32))
    qp = q_pos[:, :, None]
    kp = k_pos[:, None, :]
    dist = jnp.abs(qp - kp).astype(jnp.float32)
    dchunk = qp // CHUNK - kp // CHUNK
    valid = (kp >= 0) & (dchunk >= 0) & (dchunk <= WINDOW_CHUNKS)
    s = s - alibi_slopes()[None, None, :, :, None, None] * dist[None, :, None, None]
    s = jnp.where(valid[None, :, None, None], s, NEG_INF)
    sink = sinks.astype(jnp.float32).reshape(N_KV_HEADS, GROUP)[None, None, :, :, None, None]
    m = jnp.maximum(jnp.max(s, axis=-1, keepdims=True), sink)
    p = jnp.exp(s - m)
    denom = jnp.sum(p, axis=-1, keepdims=True) + jnp.exp(sink - m)
    o = jnp.einsum('bnkgqs,bnskd->bnqkgd', p / denom, v.astype(jnp.float32))
    return o.astype(q.dtype)


def banded_context(k, v):
    B, T = k.shape[:2]
    nc = T // CHUNK
    pad = WINDOW_CHUNKS * CHUNK
    kp = jnp.pad(k, ((0, 0), (pad, 0), (0, 0), (0, 0))).reshape(B, nc + WINDOW_CHUNKS, CHUNK, N_KV_HEADS, HEAD_DIM)
    vp = jnp.pad(v, ((0, 0), (pad, 0), (0, 0), (0, 0))).reshape(B, nc + WINDOW_CHUNKS, CHUNK, N_KV_HEADS, HEAD_DIM)
    kb = jnp.concatenate([kp[:, j:j + nc] for j in range(WINDOW_CHUNKS + 1)], axis=2)
    vb = jnp.concatenate([vp[:, j:j + nc] for j in range(WINDOW_CHUNKS + 1)], axis=2)
    q_pos = jnp.arange(T, dtype=jnp.int32).reshape(nc, CHUNK)
    k_pos = ((jnp.arange(nc, dtype=jnp.int32)[:, None] - WINDOW_CHUNKS) * CHUNK
             + jnp.arange((WINDOW_CHUNKS + 1) * CHUNK, dtype=jnp.int32)[None, :])
    return kb, vb, q_pos, k_pos


def sample_context(k_all, v_all, n_new):
    L = k_all.shape[1]
    q_pos = (PAST_LEN + jnp.arange(n_new, dtype=jnp.int32))[None, :]
    k_pos = (PAST_LEN + n_new - L + jnp.arange(L, dtype=jnp.int32))[None, :]
    return k_all[:, None], v_all[:, None], q_pos, k_pos


def swa_layer(x, ctx, pre_g, w_in, sinks, w_out, post_g):
    kb, vb, q_pos, k_pos = ctx
    B, T = x.shape[:2]
    h = rmsnorm(x, pre_g)
    z = h @ w_in
    q, gate = jnp.split(z, 2, axis=-1)
    q = q.reshape(B, q_pos.shape[0], q_pos.shape[1], N_KV_HEADS, GROUP, HEAD_DIM)
    o = sink_alibi_attention(q, kb, vb, q_pos, k_pos, sinks).reshape(B, T, ATTN_DIM)
    y = (o * jax.nn.silu(gate)) @ w_out
    return x + rmsnorm(y, post_g)


def setup_inputs(seed: int = 0) -> dict:
    key = jax.random.key(seed)
    ks = jax.random.split(key, 24)
    f32 = jnp.float32

    def nrm(k, shape, scale):
        return jax.random.normal(k, shape, f32) * scale

    rows = min(WINDOW, PAST_LEN)
    return {
        'x_prompt': nrm(ks[0], (BATCH, SEQ, D_MODEL), 1.0),
        'x_sample': nrm(ks[1], (DEC_BATCH, DEC_SEQ, D_MODEL), 1.0),
        'state_conv': nrm(ks[2], (N_A, DEC_BATCH, CONV_WIDTH - 1, D_CONV), 0.5),
        'cache_k': nrm(ks[3], (DEC_BATCH, rows, N_KV_HEADS, HEAD_DIM), 1.0),
        'cache_v': nrm(ks[4], (DEC_BATCH, rows, N_KV_HEADS, HEAD_DIM), 1.0),
        'a_pre_g': 1.0 + nrm(ks[5], (N_A, D_MODEL), 0.05),
        'a_w_in': nrm(ks[6], (N_A, D_MODEL, 3 * D_CONV), D_MODEL ** -0.5),
        'a_b_in': nrm(ks[7], (N_A, 3 * D_CONV), 0.02),
        'a_w_dw': nrm(ks[8], (N_A, CONV_WIDTH, D_CONV), CONV_WIDTH ** -0.5),
        'a_b_dw': nrm(ks[9], (N_A, D_CONV), 0.02),
        'a_ln_g': 1.0 + nrm(ks[10], (N_A, D_CONV), 0.05),
        'a_ln_b': nrm(ks[11], (N_A, D_CONV), 0.02),
        'a_w_out': nrm(ks[12], (N_A, D_CONV, D_MODEL), D_CONV ** -0.5),
        'a_b_out': nrm(ks[13], (N_A, D_MODEL), 0.02),
        'a_post_g': 1.0 + nrm(ks[14], (N_A, D_MODEL), 0.05),
        'kv_g': 1.0 + nrm(ks[15], (D_MODEL,), 0.05),
        'w_kv': nrm(ks[16], (D_MODEL, 2 * KV_DIM), D_MODEL ** -0.5),
        'b_pre_g': 1.0 + nrm(ks[17], (N_B, D_MODEL), 0.05),
        'b_w_in': nrm(ks[18], (N_B, D_MODEL, 2 * ATTN_DIM), D_MODEL ** -0.5),
        'b_sinks': nrm(ks[19], (N_B, N_HEADS), 0.5),
        'b_w_out': nrm(ks[20], (N_B, ATTN_DIM, D_MODEL), ATTN_DIM ** -0.5),
        'b_post_g': 1.0 + nrm(ks[21], (N_B, D_MODEL), 0.05),
    }


def reference(x_prompt, x_sample, state_conv, cache_k, cache_v,
              a_pre_g, a_w_in, a_b_in, a_w_dw, a_b_dw, a_ln_g, a_ln_b, a_w_out, a_b_out, a_post_g,
              kv_g, w_kv, b_pre_g, b_w_in, b_sinks, b_w_out, b_post_g):

    def trunk(x, conv_hist, k_prefix, v_prefix):
        B, T = x.shape[:2]
        new_hist = []
        ctx = None
        k_all = None
        v_all = None
        for layer in range(DEPTH):
            if layer < N_A:
                i = layer
                x, h = conformer_conv_layer(x, conv_hist[i], a_pre_g[i], a_w_in[i], a_b_in[i],
                                            a_w_dw[i], a_b_dw[i], a_ln_g[i], a_ln_b[i],
                                            a_w_out[i], a_b_out[i], a_post_g[i])
                new_hist.append(h)
                if layer == N_A - 1:
                    kv = rmsnorm(x, kv_g) @ w_kv
                    k = kv[..., :KV_DIM].reshape(B, T, N_KV_HEADS, HEAD_DIM)
                    v = kv[..., KV_DIM:].reshape(B, T, N_KV_HEADS, HEAD_DIM)
                    if k_prefix is None:
                        k_all, v_all = k, v
                        ctx = banded_context(k, v)
                    else:
                        k_all = jnp.concatenate([k_prefix.astype(k.dtype), k], axis=1)
                        v_all = jnp.concatenate([v_prefix.astype(v.dtype), v], axis=1)
                        ctx = sample_context(k_all, v_all, T)
            else:
                j = layer - N_A
                x = swa_layer(x, ctx, b_pre_g[j], b_w_in[j], b_sinks[j], b_w_out[j], b_post_g[j])
        return x, jnp.stack(new_hist, axis=0), k_all[:, -WINDOW:], v_all[:, -WINDOW:]

    zero_hist = jnp.zeros((N_A, x_prompt.shape[0], CONV_WIDTH - 1, D_CONV), x_prompt.dtype)
    y_prompt, conv_p, k_p, v_p = trunk(x_prompt, zero_hist, None, None)
    y_sample, conv_s, k_s, v_s = trunk(x_sample, state_conv.astype(x_sample.dtype), cache_k, cache_v)
    return (y_prompt, y_sample, conv_p, conv_s, k_p, v_p, k_s, v_s)
```

```python
import functools

import jax
import jax.numpy as jnp
from jax import lax
from jax.experimental import pallas as pl
from jax.experimental.pallas import tpu as pltpu

D = 1024
CHUNK = 64
CONV_WIDTH = 31
HIST = CONV_WIDTH - 1
N_HEADS = 16
N_KV = 2
GROUP = N_HEADS // N_KV
HEAD_DIM = 64
WINDOW = 128
EPS = 1e-6
NEG_INF = -1e30

LANES = 128
SUB = 8
TM = 512
NC = TM // CHUNK
HPAD = 32
KEYS = 256
QROWS = GROUP * CHUNK
RB = 256
TB = 16
VMEM_LIMIT = 56 * 1024 * 1024

f32 = jnp.float32
bf16 = jnp.bfloat16


def _sigmoid(x):
    return 1.0 / (1.0 + jnp.exp(-x))


def _rstd(x):
    return lax.rsqrt(jnp.mean(x * x, axis=-1, keepdims=True) + EPS)


def _dup_heads(t):
    lane = lax.broadcasted_iota(jnp.int32, t.shape, 1)
    r = pltpu.roll(t, HEAD_DIM, axis=1)
    lo = lane < HEAD_DIM
    return jnp.where(lo, t, r), jnp.where(lo, r, t)


def _trunk_kernel(seg_rows, first_bias,
                  x_ref, hist_ref, kpre_ref, vpre_ref,
                  a_pre_g, a_w_in, a_b_in, a_w_dw, a_b_dw, a_ln_g, a_ln_b,
                  a_w_out, a_b_out, a_post_g, kv_g, w_kv, b_pre_g, b_w_in,
                  bias_ref, b_w_out, b_post_g,
                  y_ref, tail_ref, kv_ref,
                  uext, cconv, sgate, cbuf, x1buf, kd, vd, qs, g2buf, obuf):
    nseg = TM // seg_rows
    cps = seg_rows // CHUNK
    step = pl.program_id(0)

    useg = (seg_rows + HPAD) * SUB

    if hist_ref is None:
        @pl.when(step == 0)
        def _():
            uext[0:HPAD * SUB, :] = jnp.zeros((HPAD * SUB, LANES), f32)
            for h in range(N_KV):
                kd[h, :, 0:WINDOW, :] = jnp.zeros((nseg, WINDOW, LANES), bf16)
                vd[h, :, 0:WINDOW, :] = jnp.zeros((nseg, WINDOW, LANES), bf16)

        @pl.when(step > 0)
        def _():
            uext[0:HPAD * SUB, :] = uext[seg_rows * SUB:useg, :]
            for h in range(N_KV):
                kd[h, :, 0:WINDOW, :] = kd[h, :, seg_rows:seg_rows + WINDOW, :]
                vd[h, :, 0:WINDOW, :] = vd[h, :, seg_rows:seg_rows + WINDOW, :]
    else:
        for s in range(nseg):
            uext[s * useg:s * useg + HPAD * SUB, :] = hist_ref[s]
        k0, k1 = _dup_heads(kpre_ref[...].reshape(nseg * WINDOW, LANES))
        v0, v1 = _dup_heads(vpre_ref[...].reshape(nseg * WINDOW, LANES))
        kd[0, :, 0:WINDOW, :] = k0.astype(bf16).reshape(nseg, WINDOW, LANES)
        kd[1, :, 0:WINDOW, :] = k1.astype(bf16).reshape(nseg, WINDOW, LANES)
        vd[0, :, 0:WINDOW, :] = v0.astype(bf16).reshape(nseg, WINDOW, LANES)
        vd[1, :, 0:WINDOW, :] = v1.astype(bf16).reshape(nseg, WINDOW, LANES)

    def a_front(r, carry):
        r0 = pl.multiple_of(r * RB, RB)
        x = x_ref[pl.ds(r0, RB), :]
        h = (x * _rstd(x) * a_pre_g[...]).astype(bf16)
        z = jnp.dot(h, a_w_in[...], preferred_element_type=f32) + a_b_in[...]
        u = z[:, 0:D] * _sigmoid(z[:, D:2 * D])
        gate = z[:, 2 * D:3 * D]
        sgate[pl.ds(r0, RB), :] = gate * _sigmoid(gate)
        for s in range(RB // seg_rows if nseg > 1 else 1):
            rows = seg_rows if nseg > 1 else RB
            if nseg > 1:
                base = (r * (RB // seg_rows) + s) * useg + HPAD * SUB
            else:
                base = (HPAD + r0) * SUB
            for k in range(SUB):
                uext[pl.ds(base + k, rows, stride=SUB), :] = (
                    u[s * rows:(s + 1) * rows, k * LANES:(k + 1) * LANES])
        return carry

    lax.fori_loop(0, TM // RB, a_front, 0)

    def a_conv(i, carry):
        t0 = i * TB
        seg = t0 // seg_rows
        base = seg * useg + (t0 - seg * seg_rows + HPAD - HIST) * SUB
        rows = [uext[pl.ds(pl.multiple_of(base + d * SUB, SUB), SUB), :]
                for d in range(TB + HIST)]
        accs = [None] * TB
        for j in range(CONV_WIDTH):
            wj = a_w_dw[j * SUB:(j + 1) * SUB, :]
            for t in range(TB):
                term = rows[t + j] * wj
                accs[t] = term if j == 0 else accs[t] + term
        for t in range(TB):
            cconv[pl.ds(pl.multiple_of((t0 + t) * SUB, SUB), SUB), :] = accs[t]
        return carry

    lax.fori_loop(0, TM // TB, a_conv, 0)

    def a_norm(c, carry):
        t0 = pl.multiple_of(c * CHUNK, CHUNK)
        cv = jnp.concatenate(
            [cconv[pl.ds(t0 * SUB + k, CHUNK, stride=SUB), :] for k in range(SUB)],
            axis=1) + a_b_dw[...]
        mu = jnp.mean(cv, axis=-1, keepdims=True)
        xc = cv - mu
        ln = xc * lax.rsqrt(jnp.mean(xc * xc, axis=-1, keepdims=True) + EPS)
        ln = ln * a_ln_g[...] + a_ln_b[...]
        act = ln * _sigmoid(ln) * sgate[pl.ds(t0, CHUNK), :]
        cbuf[pl.ds(t0, CHUNK), :] = act.astype(bf16)
        return carry

    lax.fori_loop(0, NC, a_norm, 0)

    for s in range(nseg):
        tail_ref[s] = uext[s * useg + seg_rows * SUB:(s + 1) * useg, :]

    lane = lax.broadcasted_iota(jnp.int32, (RB, LANES), 1)
    lo_mask = lane < HEAD_DIM

    def a_back(r, carry):
        r0 = pl.multiple_of(r * RB, RB)
        y = jnp.dot(cbuf[pl.ds(r0, RB), :], a_w_out[...],
                    preferred_element_type=f32) + a_b_out[...]
        x1 = x_ref[pl.ds(r0, RB), :] + y * _rstd(y) * a_post_g[...]
        x1buf[pl.ds(r0, RB), :] = x1
        xn = x1 * _rstd(x1)
        kv = jnp.dot((xn * kv_g[...]).astype(bf16), w_kv[...],
                     preferred_element_type=f32)
        kv_ref[pl.ds(r0, RB), :] = kv
        k0, k1 = _dup_heads(kv[:, 0:LANES])
        v0, v1 = _dup_heads(kv[:, LANES:2 * LANES])
        for hh, (kk, vv) in enumerate(((k0, v0), (k1, v1))):
            kk = kk.astype(bf16)
            vv = vv.astype(bf16)
            if nseg == 1:
                kd[hh, 0, pl.ds(WINDOW + r0, RB), :] = kk
                vd[hh, 0, pl.ds(WINDOW + r0, RB), :] = vv
            else:
                for s in range(RB // seg_rows):
                    sl = slice(s * seg_rows, (s + 1) * seg_rows)
                    kd[hh, r * (RB // seg_rows) + s, WINDOW:WINDOW + seg_rows, :] = kk[sl]
                    vd[hh, r * (RB // seg_rows) + s, WINDOW:WINDOW + seg_rows, :] = vv[sl]
        z2 = jnp.dot((xn * b_pre_g[...]).astype(bf16), b_w_in[...],
                     preferred_element_type=f32)
        g2 = z2[:, D:2 * D]
        g2buf[pl.ds(r0, RB), :] = g2 * _sigmoid(g2)
        q = z2[:, 0:D] * (HEAD_DIM ** -0.5)
        for blk in range(D // LANES):
            qb = q[:, blk * LANES:(blk + 1) * LANES]
            kh, b = divmod(blk, GROUP // 2)
            lo = jnp.where(lo_mask, qb, 0.0).astype(bf16).reshape(RB // CHUNK, CHUNK, LANES)
            hi = jnp.where(lo_mask, 0.0, qb).astype(bf16).reshape(RB // CHUNK, CHUNK, LANES)
            c0 = r * (RB // CHUNK)
            qs[kh, pl.ds(c0, RB // CHUNK), b * LANES:b * LANES + CHUNK, :] = lo
            qs[kh, pl.ds(c0, RB // CHUNK), b * LANES + CHUNK:(b + 1) * LANES, :] = hi
        return carry

    lax.fori_loop(0, TM // RB, a_back, 0)

    zkeys = jnp.zeros((KEYS - WINDOW - CHUNK, LANES), bf16)
    lo64 = lax.broadcasted_iota(jnp.int32, (CHUNK, LANES), 1) < HEAD_DIM

    def attn(c, carry):
        seg = c // cps
        cl = c % cps
        row0 = cl * CHUNK
        t0 = pl.multiple_of(c * CHUNK, CHUNK)
        if hist_ref is None:
            var = jnp.minimum(step * NC + c + first_bias, 2)
        else:
            var = jnp.minimum(cl + first_bias, 2)
        outs = []
        for kh in range(N_KV):
            kx = jnp.concatenate(
                [kd[kh, seg, pl.ds(row0, WINDOW + CHUNK), :], zkeys], axis=0)
            vx = jnp.concatenate(
                [vd[kh, seg, pl.ds(row0, WINDOW + CHUNK), :], zkeys], axis=0)
            s = lax.dot_general(qs[kh, c], kx, (((1,), (1,)), ((), ())),
                                preferred_element_type=f32)
            s = s + bias_ref[var, kh]
            m = jnp.max(s, axis=-1, keepdims=True)
            p = jnp.exp(s - m)
            l = jnp.sum(p, axis=-1, keepdims=True)
            o = jnp.dot(p.astype(bf16), vx, preferred_element_type=f32)
            o = o * (1.0 / l)
            for b in range(GROUP // 2):
                outs.append(jnp.where(lo64,
                                      o[b * LANES:b * LANES + CHUNK, :],
                                      o[b * LANES + CHUNK:(b + 1) * LANES, :]))
        oc = jnp.concatenate(outs, axis=1)
        obuf[pl.ds(t0, CHUNK), :] = (oc * g2buf[pl.ds(t0, CHUNK), :]).astype(bf16)
        return carry

    lax.fori_loop(0, NC, attn, 0)

    def b_back(r, carry):
        r0 = pl.multiple_of(r * RB, RB)
        y = jnp.dot(obuf[pl.ds(r0, RB), :], b_w_out[...], preferred_element_type=f32)
        y_ref[pl.ds(r0, RB), :] = x1buf[pl.ds(r0, RB), :] + y * _rstd(y) * b_post_g[...]
        return carry

    lax.fori_loop(0, TM // RB, b_back, 0)


def _attn_bias(sinks):
    h = jnp.arange(1, N_HEADS + 1, dtype=f32)
    slopes = (2.0 ** (-8.0 * h / N_HEADS)).reshape(N_KV, GROUP, 1, 1)
    i = jnp.arange(CHUNK, dtype=jnp.int32)[:, None]
    j = jnp.arange(KEYS, dtype=jnp.int32)[None, :]
    dist = jnp.abs(WINDOW + i - j).astype(f32)[None, None]
    nkeys = WINDOW + CHUNK
    jj = j[None, None]
    base = jnp.where(jj < nkeys, -(slopes * dist),
                     jnp.where(jj == nkeys,
                               sinks.astype(f32).reshape(N_KV, GROUP, 1, 1), NEG_INF))
    variants = [jnp.where(jj < WINDOW - v * CHUNK, NEG_INF, base) for v in range(3)]
    return jnp.stack(variants, axis=0).reshape(3, N_KV, QROWS, KEYS)


def _const_spec(shape):
    nd = len(shape)
    return pl.BlockSpec(shape, lambda i, _nd=nd: (0,) * _nd,
                        pipeline_mode=pl.Buffered(1))


def _trunk(x2d, hist, kpre, vpre, weights, bias, seg_rows, first_bias):
    T = x2d.shape[0]
    nseg = TM // seg_rows
    n_steps = T // TM
    has_state = hist is not None

    in_specs = [pl.BlockSpec((TM, D), lambda i: (i, 0))]
    args = [x2d]
    if has_state:
        in_specs += [pl.BlockSpec((nseg, HPAD * SUB, LANES), lambda i: (i, 0, 0)),
                     pl.BlockSpec((nseg, WINDOW, LANES), lambda i: (i, 0, 0)),
                     pl.BlockSpec((nseg, WINDOW, LANES), lambda i: (i, 0, 0))]
        args += [hist, kpre, vpre]
    w_list = list(weights[:-2]) + [bias] + list(weights[-2:])
    in_specs += [_const_spec(w.shape) for w in w_list]
    args += w_list

    out_shape = (jax.ShapeDtypeStruct((T, D), f32),
                 jax.ShapeDtypeStruct((n_steps * nseg, HPAD * SUB, LANES), f32),
                 jax.ShapeDtypeStruct((T, 2 * LANES), f32))
    out_specs = (pl.BlockSpec((TM, D), lambda i: (i, 0)),
                 pl.BlockSpec((nseg, HPAD * SUB, LANES), lambda i: (i, 0, 0)),
                 pl.BlockSpec((TM, 2 * LANES), lambda i: (i, 0)))
    scratch = [
        pltpu.VMEM((nseg * (seg_rows + HPAD) * SUB, LANES), f32),
        pltpu.VMEM((TM * SUB, LANES), f32),
        pltpu.VMEM((TM, D), f32),
        pltpu.VMEM((TM, D), bf16),
        pltpu.VMEM((TM, D), f32),
        pltpu.VMEM((N_KV, nseg, seg_rows + WINDOW, LANES), bf16),
        pltpu.VMEM((N_KV, nseg, seg_rows + WINDOW, LANES), bf16),
        pltpu.VMEM((N_KV, NC, QROWS, LANES), bf16),
        pltpu.VMEM((TM, D), f32),
        pltpu.VMEM((TM, D), bf16),
    ]

    if has_state:
        body = functools.partial(_trunk_kernel, seg_rows, first_bias)
    else:
        def body(x_ref, *rest):
            _trunk_kernel(seg_rows, first_bias, x_ref, None, None, None, *rest)

    return pl.pallas_call(
        body,
        grid=(n_steps,),
        in_specs=in_specs,
        out_specs=out_specs,
        out_shape=out_shape,
        scratch_shapes=scratch,
        compiler_params=pltpu.CompilerParams(
            dimension_semantics=("arbitrary",),
            vmem_limit_bytes=VMEM_LIMIT),
        name="trunk_sample" if has_state else "trunk_prompt",
    )(*args)


def kernel(x_prompt, x_sample, state_conv, cache_k, cache_v, a_pre_g, a_w_in, a_b_in, a_w_dw, a_b_dw, a_ln_g, a_ln_b, a_w_out, a_b_out, a_post_g, kv_g, w_kv, b_pre_g, b_w_in, b_sinks, b_w_out, b_post_g):
    row = lambda v: v.reshape(1, -1).astype(f32)
    weights = (row(a_pre_g[0]), a_w_in[0].astype(bf16), row(a_b_in[0]),
               a_w_dw[0].astype(f32).reshape(CONV_WIDTH * SUB, LANES), row(a_b_dw[0]), row(a_ln_g[0]), row(a_ln_b[0]),
               a_w_out[0].astype(bf16), row(a_b_out[0]), row(a_post_g[0]),
               row(kv_g), w_kv.astype(bf16), row(b_pre_g[0]), b_w_in[0].astype(bf16),
               b_w_out[0].astype(bf16), row(b_post_g[0]))
    bias = _attn_bias(b_sinks[0])

    B, T, _ = x_prompt.shape
    yp, tail_p, kv_p = _trunk(x_prompt.reshape(B * T, D), None, None, None,
                              weights, bias, seg_rows=TM, first_bias=0)
    y_prompt = yp.reshape(B, T, D)
    conv_p = tail_p[-1:].reshape(1, HPAD, D)[:, HPAD - HIST:, :][None]
    k_p = kv_p[-WINDOW:, 0:LANES].reshape(B, WINDOW, N_KV, HEAD_DIM)
    v_p = kv_p[-WINDOW:, LANES:].reshape(B, WINDOW, N_KV, HEAD_DIM)

    SB, ST, _ = x_sample.shape
    hist = jnp.pad(state_conv[0].astype(f32),
                   ((0, 0), (HPAD - HIST, 0), (0, 0))).reshape(SB, HPAD * SUB, LANES)
    kpre = cache_k.astype(f32).reshape(SB, WINDOW, LANES)
    vpre = cache_v.astype(f32).reshape(SB, WINDOW, LANES)
    ys, tail_s, kv_s = _trunk(x_sample.reshape(SB * ST, D), hist, kpre, vpre,
                              weights, bias, seg_rows=ST, first_bias=2)
    y_sample = ys.reshape(SB, ST, D)
    conv_s = tail_s.reshape(SB, HPAD, D)[:, HPAD - HIST:, :][None]
    k_new = kv_s[:, 0:LANES].reshape(SB, ST, N_KV, HEAD_DIM)
    v_new = kv_s[:, LANES:].reshape(SB, ST, N_KV, HEAD_DIM)
    k_s = jnp.concatenate([cache_k.astype(f32)[:, ST:], k_new], axis=1)
    v_s = jnp.concatenate([cache_v.astype(f32)[:, ST:], v_new], axis=1)
    return (y_prompt, y_sample, conv_p, conv_s, k_p, v_p, k_s, v_s)
```

```python
import functools

import jax
import jax.numpy as jnp
from jax import lax
from jax.experimental import pallas as pl
from jax.experimental.pallas import tpu as pltpu

D = 1024
CHUNK = 64
CONV_WIDTH = 31
HIST = CONV_WIDTH - 1
N_HEADS = 16
N_KV = 2
GROUP = N_HEADS // N_KV
HEAD_DIM = 64
WINDOW = 128
EPS = 1e-6
NEG_INF = -1e30

LANES = 128
SUB = 8
TM = 512
NC = TM // CHUNK
HPAD = 32
KEYS = 256
QROWS = GROUP * CHUNK
RB = 512
TB = 16
VMEM_LIMIT = 56 * 1024 * 1024

f32 = jnp.float32
bf16 = jnp.bfloat16


def _sigmoid(x):
    return 1.0 / (1.0 + jnp.exp(-x))


def _rstd(x):
    return lax.rsqrt(jnp.mean(x * x, axis=-1, keepdims=True) + EPS)


def _dup_heads(t):
    lane = lax.broadcasted_iota(jnp.int32, t.shape, 1)
    r = pltpu.roll(t, HEAD_DIM, axis=1)
    lo = lane < HEAD_DIM
    return jnp.where(lo, t, r), jnp.where(lo, r, t)


def _trunk_kernel(seg_rows, first_bias,
                  x_ref, hist_ref, kpre_ref, vpre_ref,
                  a_pre_g, a_w_in, a_b_in, a_w_dw, a_b_dw, a_ln_g, a_ln_b,
                  a_w_out, a_b_out, a_post_g, kv_g, w_kv, b_pre_g, b_w_in,
                  bias_ref, b_w_out, b_post_g,
                  y_ref, tail_ref, kv_ref,
                  uext, cconv, sgate, cbuf, x1buf, kd, vd, qs, g2buf, obuf):
    nseg = TM // seg_rows
    cps = seg_rows // CHUNK
    step = pl.program_id(0)

    useg = (seg_rows + HPAD) * SUB

    if hist_ref is None:
        @pl.when(step == 0)
        def _():
            uext[0:HPAD * SUB, :] = jnp.zeros((HPAD * SUB, LANES), f32)
            for h in range(N_KV):
                kd[h, :, 0:WINDOW, :] = jnp.zeros((nseg, WINDOW, LANES), bf16)
                vd[h, :, 0:WINDOW, :] = jnp.zeros((nseg, WINDOW, LANES), bf16)

        @pl.when(step > 0)
        def _():
            uext[0:HPAD * SUB, :] = uext[seg_rows * SUB:useg, :]
            for h in range(N_KV):
                kd[h, :, 0:WINDOW, :] = kd[h, :, seg_rows:seg_rows + WINDOW, :]
                vd[h, :, 0:WINDOW, :] = vd[h, :, seg_rows:seg_rows + WINDOW, :]
    else:
        for s in range(nseg):
            uext[s * useg:s * useg + HPAD * SUB, :] = hist_ref[s]
        k0, k1 = _dup_heads(kpre_ref[...].reshape(nseg * WINDOW, LANES))
        v0, v1 = _dup_heads(vpre_ref[...].reshape(nseg * WINDOW, LANES))
        kd[0, :, 0:WINDOW, :] = k0.astype(bf16).reshape(nseg, WINDOW, LANES)
        kd[1, :, 0:WINDOW, :] = k1.astype(bf16).reshape(nseg, WINDOW, LANES)
        vd[0, :, 0:WINDOW, :] = v0.astype(bf16).reshape(nseg, WINDOW, LANES)
        vd[1, :, 0:WINDOW, :] = v1.astype(bf16).reshape(nseg, WINDOW, LANES)

    def a_front(r, carry):
        r0 = pl.multiple_of(r * RB, RB)
        x = x_ref[pl.ds(r0, RB), :]
        h = (x * _rstd(x) * a_pre_g[...]).astype(bf16)
        z = jnp.dot(h, a_w_in[...], preferred_element_type=f32) + a_b_in[...]
        u = z[:, 0:D] * _sigmoid(z[:, D:2 * D])
        gate = z[:, 2 * D:3 * D]
        sgate[pl.ds(r0, RB), :] = gate * _sigmoid(gate)
        for s in range(RB // seg_rows if nseg > 1 else 1):
            rows = seg_rows if nseg > 1 else RB
            if nseg > 1:
                base = (r * (RB // seg_rows) + s) * useg + HPAD * SUB
            else:
                base = (HPAD + r0) * SUB
            for k in range(SUB):
                uext[pl.ds(base + k, rows, stride=SUB), :] = (
                    u[s * rows:(s + 1) * rows, k * LANES:(k + 1) * LANES])
        return carry

    lax.fori_loop(0, TM // RB, a_front, 0)

    def a_conv(i, carry):
        t0 = i * TB
        seg = t0 // seg_rows
        base = seg * useg + (t0 - seg * seg_rows + HPAD - HIST) * SUB
        rows = [uext[pl.ds(pl.multiple_of(base + d * SUB, SUB), SUB), :]
                for d in range(TB + HIST)]
        accs = [a_b_dw[...]] * TB
        for j in range(CONV_WIDTH):
            wj = a_w_dw[j * SUB:(j + 1) * SUB, :]
            for t in range(TB):
                accs[t] = accs[t] + rows[t + j] * wj
        for t in range(TB):
            cconv[pl.ds(pl.multiple_of((t0 + t) * SUB, SUB), SUB), :] = accs[t]
        return carry

    lax.fori_loop(0, TM // TB, a_conv, 0, unroll=TM // TB)

    def a_norm(c, carry):
        t0 = pl.multiple_of(c * CHUNK, CHUNK)
        cv = jnp.concatenate(
            [cconv[pl.ds(t0 * SUB + k, CHUNK, stride=SUB), :] for k in range(SUB)],
            axis=1)
        mu = jnp.mean(cv, axis=-1, keepdims=True)
        xc = cv - mu
        ln = xc * lax.rsqrt(jnp.mean(xc * xc, axis=-1, keepdims=True) + EPS)
        ln = ln * a_ln_g[...] + a_ln_b[...]
        act = ln * _sigmoid(ln) * sgate[pl.ds(t0, CHUNK), :]
        cbuf[pl.ds(t0, CHUNK), :] = act.astype(bf16)
        return carry

    lax.fori_loop(0, NC, a_norm, 0, unroll=8)

    for s in range(nseg):
        tail_ref[s] = uext[s * useg + seg_rows * SUB:(s + 1) * useg, :]

    lane = lax.broadcasted_iota(jnp.int32, (RB, LANES), 1)
    lo_mask = lane < HEAD_DIM

    def a_back(r, carry):
        r0 = pl.multiple_of(r * RB, RB)
        y = jnp.dot(cbuf[pl.ds(r0, RB), :], a_w_out[...],
                    preferred_element_type=f32) + a_b_out[...]
        x1 = x_ref[pl.ds(r0, RB), :] + y * _rstd(y) * a_post_g[...]
        x1buf[pl.ds(r0, RB), :] = x1
        xn = x1 * _rstd(x1)
        kv = jnp.dot((xn * kv_g[...]).astype(bf16), w_kv[...],
                     preferred_element_type=f32)
        kv_ref[pl.ds(r0, RB), :] = kv
        k0, k1 = _dup_heads(kv[:, 0:LANES])
        v0, v1 = _dup_heads(kv[:, LANES:2 * LANES])
        for hh, (kk, vv) in enumerate(((k0, v0), (k1, v1))):
            kk = kk.astype(bf16)
            vv = vv.astype(bf16)
            if nseg == 1:
                kd[hh, 0, pl.ds(WINDOW + r0, RB), :] = kk
                vd[hh, 0, pl.ds(WINDOW + r0, RB), :] = vv
            else:
                for s in range(RB // seg_rows):
                    sl = slice(s * seg_rows, (s + 1) * seg_rows)
                    kd[hh, r * (RB // seg_rows) + s, WINDOW:WINDOW + seg_rows, :] = kk[sl]
                    vd[hh, r * (RB // seg_rows) + s, WINDOW:WINDOW + seg_rows, :] = vv[sl]
        z2 = jnp.dot((xn * b_pre_g[...]).astype(bf16), b_w_in[...],
                     preferred_element_type=f32)
        g2 = z2[:, D:2 * D]
        g2buf[pl.ds(r0, RB), :] = g2 * _sigmoid(g2)
        q = z2[:, 0:D] * (HEAD_DIM ** -0.5)
        for blk in range(D // LANES):
            qb = q[:, blk * LANES:(blk + 1) * LANES]
            kh, b = divmod(blk, GROUP // 2)
            lo = jnp.where(lo_mask, qb, 0.0).astype(bf16).reshape(RB // CHUNK, CHUNK, LANES)
            hi = jnp.where(lo_mask, 0.0, qb).astype(bf16).reshape(RB // CHUNK, CHUNK, LANES)
            c0 = r * (RB // CHUNK)
            qs[kh, pl.ds(c0, RB // CHUNK), b * LANES:b * LANES + CHUNK, :] = lo
            qs[kh, pl.ds(c0, RB // CHUNK), b * LANES + CHUNK:(b + 1) * LANES, :] = hi
        return carry

    lax.fori_loop(0, TM // RB, a_back, 0)

    zkeys = jnp.zeros((KEYS - WINDOW - CHUNK, LANES), bf16)
    lo64 = lax.broadcasted_iota(jnp.int32, (CHUNK, LANES), 1) < HEAD_DIM

    def attn(c, carry):
        seg = c // cps
        cl = c % cps
        row0 = cl * CHUNK
        t0 = pl.multiple_of(c * CHUNK, CHUNK)
        if hist_ref is None:
            var = jnp.minimum(step * NC + c + first_bias, 2)
        else:
            var = jnp.minimum(cl + first_bias, 2)
        outs = []
        for kh in range(N_KV):
            kx = jnp.concatenate(
                [kd[kh, seg, pl.ds(row0, WINDOW + CHUNK), :], zkeys], axis=0)
            vx = jnp.concatenate(
                [vd[kh, seg, pl.ds(row0, WINDOW + CHUNK), :], zkeys], axis=0)
            s = lax.dot_general(qs[kh, c], kx, (((1,), (1,)), ((), ())),
                                preferred_element_type=f32)
            s = s + bias_ref[var, kh]
            m = jnp.max(s, axis=-1, keepdims=True)
            p = jnp.exp(s - m)
            l = jnp.sum(p, axis=-1, keepdims=True)
            o = jnp.dot(p.astype(bf16), vx, preferred_element_type=f32)
            o = o * (1.0 / l)
            for b in range(GROUP // 2):
                outs.append(jnp.where(lo64,
                                      o[b * LANES:b * LANES + CHUNK, :],
                                      o[b * LANES + CHUNK:(b + 1) * LANES, :]))
        oc = jnp.concatenate(outs, axis=1)
        obuf[pl.ds(t0, CHUNK), :] = (oc * g2buf[pl.ds(t0, CHUNK), :]).astype(bf16)
        return carry

    lax.fori_loop(0, NC, attn, 0, unroll=8)

    def b_back(r, carry):
        r0 = pl.multiple_of(r * RB, RB)
        y = jnp.dot(obuf[pl.ds(r0, RB), :], b_w_out[...], preferred_element_type=f32)
        y_ref[pl.ds(r0, RB), :] = x1buf[pl.ds(r0, RB), :] + y * _rstd(y) * b_post_g[...]
        return carry

    lax.fori_loop(0, TM // RB, b_back, 0)


def _attn_bias(sinks):
    h = jnp.arange(1, N_HEADS + 1, dtype=f32)
    slopes = (2.0 ** (-8.0 * h / N_HEADS)).reshape(N_KV, GROUP, 1, 1)
    i = jnp.arange(CHUNK, dtype=jnp.int32)[:, None]
    j = jnp.arange(KEYS, dtype=jnp.int32)[None, :]
    dist = jnp.abs(WINDOW + i - j).astype(f32)[None, None]
    nkeys = WINDOW + CHUNK
    jj = j[None, None]
    base = jnp.where(jj < nkeys, -(slopes * dist),
                     jnp.where(jj == nkeys,
                               sinks.astype(f32).reshape(N_KV, GROUP, 1, 1), NEG_INF))
    variants = [jnp.where(jj < WINDOW - v * CHUNK, NEG_INF, base) for v in range(3)]
    return jnp.stack(variants, axis=0).reshape(3, N_KV, QROWS, KEYS)


def _const_spec(shape):
    nd = len(shape)
    return pl.BlockSpec(shape, lambda i, _nd=nd: (0,) * _nd,
                        pipeline_mode=pl.Buffered(1))


def _trunk(x2d, hist, kpre, vpre, weights, bias, seg_rows, first_bias):
    T = x2d.shape[0]
    nseg = TM // seg_rows
    n_steps = T // TM
    has_state = hist is not None

    in_specs = [pl.BlockSpec((TM, D), lambda i: (i, 0))]
    args = [x2d]
    if has_state:
        in_specs += [pl.BlockSpec((nseg, HPAD * SUB, LANES), lambda i: (i, 0, 0)),
                     pl.BlockSpec((nseg, WINDOW, LANES), lambda i: (i, 0, 0)),
                     pl.BlockSpec((nseg, WINDOW, LANES), lambda i: (i, 0, 0))]
        args += [hist, kpre, vpre]
    w_list = list(weights[:-2]) + [bias] + list(weights[-2:])
    in_specs += [_const_spec(w.shape) for w in w_list]
    args += w_list

    out_shape = (jax.ShapeDtypeStruct((T, D), f32),
                 jax.ShapeDtypeStruct((n_steps * nseg, HPAD * SUB, LANES), f32),
                 jax.ShapeDtypeStruct((T, 2 * LANES), f32))
    out_specs = (pl.BlockSpec((TM, D), lambda i: (i, 0)),
                 pl.BlockSpec((nseg, HPAD * SUB, LANES), lambda i: (i, 0, 0)),
                 pl.BlockSpec((TM, 2 * LANES), lambda i: (i, 0)))
    scratch = [
        pltpu.VMEM((nseg * (seg_rows + HPAD) * SUB, LANES), f32),
        pltpu.VMEM((TM * SUB, LANES), f32),
        pltpu.VMEM((TM, D), f32),
        pltpu.VMEM((TM, D), bf16),
        pltpu.VMEM((TM, D), f32),
        pltpu.VMEM((N_KV, nseg, seg_rows + WINDOW, LANES), bf16),
        pltpu.VMEM((N_KV, nseg, seg_rows + WINDOW, LANES), bf16),
        pltpu.VMEM((N_KV, NC, QROWS, LANES), bf16),
        pltpu.VMEM((TM, D), f32),
        pltpu.VMEM((TM, D), bf16),
    ]

    if has_state:
        body = functools.partial(_trunk_kernel, seg_rows, first_bias)
    else:
        def body(x_ref, *rest):
            _trunk_kernel(seg_rows, first_bias, x_ref, None, None, None, *rest)

    return pl.pallas_call(
        body,
        grid=(n_steps,),
        in_specs=in_specs,
        out_specs=out_specs,
        out_shape=out_shape,
        scratch_shapes=scratch,
        compiler_params=pltpu.CompilerParams(
            dimension_semantics=("arbitrary",),
            vmem_limit_bytes=VMEM_LIMIT),
        name="trunk_sample" if has_state else "trunk_prompt",
    )(*args)


def kernel(x_prompt, x_sample, state_conv, cache_k, cache_v, a_pre_g, a_w_in, a_b_in, a_w_dw, a_b_dw, a_ln_g, a_ln_b, a_w_out, a_b_out, a_post_g, kv_g, w_kv, b_pre_g, b_w_in, b_sinks, b_w_out, b_post_g):
    row = lambda v: v.reshape(1, -1).astype(f32)
    weights = (row(a_pre_g[0]), a_w_in[0].astype(bf16), row(a_b_in[0]),
               a_w_dw[0].astype(f32).reshape(CONV_WIDTH * SUB, LANES),
               a_b_dw[0].astype(f32).reshape(SUB, LANES), row(a_ln_g[0]), row(a_ln_b[0]),
               a_w_out[0].astype(bf16), row(a_b_out[0]), row(a_post_g[0]),
               row(kv_g), w_kv.astype(bf16), row(b_pre_g[0]), b_w_in[0].astype(bf16),
               b_w_out[0].astype(bf16), row(b_post_g[0]))
    bias = _attn_bias(b_sinks[0])

    B, T, _ = x_prompt.shape
    yp, tail_p, kv_p = _trunk(x_prompt.reshape(B * T, D), None, None, None,
                              weights, bias, seg_rows=TM, first_bias=0)
    y_prompt = yp.reshape(B, T, D)
    conv_p = tail_p[-1:].reshape(1, HPAD, D)[:, HPAD - HIST:, :][None]
    k_p = kv_p[-WINDOW:, 0:LANES].reshape(B, WINDOW, N_KV, HEAD_DIM)
    v_p = kv_p[-WINDOW:, LANES:].reshape(B, WINDOW, N_KV, HEAD_DIM)

    SB, ST, _ = x_sample.shape
    hist = jnp.pad(state_conv[0].astype(f32),
                   ((0, 0), (HPAD - HIST, 0), (0, 0))).reshape(SB, HPAD * SUB, LANES)
    kpre = cache_k.astype(f32).reshape(SB, WINDOW, LANES)
    vpre = cache_v.astype(f32).reshape(SB, WINDOW, LANES)
    ys, tail_s, kv_s = _trunk(x_sample.reshape(SB * ST, D), hist, kpre, vpre,
                              weights, bias, seg_rows=ST, first_bias=2)
    y_sample = ys.reshape(SB, ST, D)
    conv_s = tail_s.reshape(SB, HPAD, D)[:, HPAD - HIST:, :][None]
    k_new = kv_s[:, 0:LANES].reshape(SB, ST, N_KV, HEAD_DIM)
    v_new = kv_s[:, LANES:].reshape(SB, ST, N_KV, HEAD_DIM)
    k_s = jnp.concatenate([cache_k.astype(f32)[:, ST:], k_new], axis=1)
    v_s = jnp.concatenate([cache_v.astype(f32)[:, ST:], v_new], axis=1)
    return (y_prompt, y_sample, conv_p, conv_s, k_p, v_p, k_s, v_s)
```

```python
import functools
import math

import jax
import jax.numpy as jnp
from jax import lax
from jax.experimental import pallas as pl
from jax.experimental.pallas import tpu as pltpu

D = 1024
CHUNK = 64
CONV_WIDTH = 31
HIST = CONV_WIDTH - 1
N_HEADS = 16
N_KV = 2
GROUP = N_HEADS // N_KV
HEAD_DIM = 64
WINDOW = 128
EPS = 1e-6
NEG_INF = -1e30
LOG2E = math.log2(math.e)

LANES = 128
SUB = 8
NBLK = D // LANES
TM = SUB * CHUNK
NC = TM // CHUNK
KEYS = 256
QROWS = GROUP * CHUNK
TB = 16
VMEM_LIMIT = 56 * 1024 * 1024

f32 = jnp.float32
bf16 = jnp.bfloat16


def _sigmoid(x):
    return 1.0 / (1.0 + jnp.exp(-x))


def _silu_bf16(x):
    hx = x * 0.5
    return hx * jnp.tanh(hx) + hx


def _rstd(x):
    return lax.rsqrt(jnp.mean(x * x, axis=-1, keepdims=True) + EPS)


def _dup_heads(t):
    lane = lax.broadcasted_iota(jnp.int32, t.shape, 1)
    r = pltpu.roll(t, HEAD_DIM, axis=1)
    lo = lane < HEAD_DIM
    return jnp.where(lo, t, r), jnp.where(lo, r, t)


def _trunk_kernel(seg_rows, first_bias,
                  x_ref, hist_ref, kpre_ref, vpre_ref,
                  a_pre_g, a_w_in, a_b_in, a_w_dw, a_b_dw, a_ln_g, a_ln_b,
                  a_w_out, a_b_out, a_post_g, kv_g, w_kv, b_pre_g, b_w_in,
                  bias_ref, b_w_out, b_post_g,
                  y_ref, tail_ref, kv_ref,
                  u3, p3, c3, sgate, cbuf, x1buf, kd, vd, qs, g2buf, oraw):
    nseg = TM // seg_rows
    cps = seg_rows // CHUNK
    step = pl.program_id(0)
    streams_on_sublanes = hist_ref is not None
    hrows = HIST * SUB

    if hist_ref is None:
        @pl.when(step == 0)
        def _():
            p3[...] = jnp.zeros(p3.shape, f32)
            kd[...] = jnp.zeros(kd.shape, bf16)
            vd[...] = jnp.zeros(vd.shape, bf16)

        for h in range(N_KV):
            kd[h, :, 0:WINDOW, :] = kd[h, :, seg_rows:seg_rows + WINDOW, :]
            vd[h, :, 0:WINDOW, :] = vd[h, :, seg_rows:seg_rows + WINDOW, :]
    else:
        k0, k1 = _dup_heads(kpre_ref[...].reshape(nseg * WINDOW, LANES))
        v0, v1 = _dup_heads(vpre_ref[...].reshape(nseg * WINDOW, LANES))
        kd[0, :, 0:WINDOW, :] = k0.astype(bf16).reshape(nseg, WINDOW, LANES)
        kd[1, :, 0:WINDOW, :] = k1.astype(bf16).reshape(nseg, WINDOW, LANES)
        vd[0, :, 0:WINDOW, :] = v0.astype(bf16).reshape(nseg, WINDOW, LANES)
        vd[1, :, 0:WINDOW, :] = v1.astype(bf16).reshape(nseg, WINDOW, LANES)

    x = x_ref[...]
    h = (x * _rstd(x) * a_pre_g[...]).astype(bf16)
    sub0 = lax.broadcasted_iota(jnp.int32, (SUB, LANES), 0) == 0

    for k in range(NBLK):
        cs = slice(2 * LANES * k, 2 * LANES * (k + 1))
        zk = jnp.dot(h, a_w_in[:, cs], preferred_element_type=f32) + a_b_in[:, cs]
        uk = zk[:, 0:LANES] * _sigmoid(zk[:, LANES:2 * LANES])
        for s in range(SUB):
            u3[pl.ds(k * TM + s, CHUNK, stride=SUB), :] = uk[s * CHUNK:(s + 1) * CHUNK, :]

        def tile_in(ap):
            if ap >= 0:
                return u3[k * TM + ap * SUB:k * TM + (ap + 1) * SUB, :]
            r0 = k * hrows + (ap + HIST) * SUB
            if streams_on_sublanes:
                return hist_ref[r0:r0 + SUB, :]
            a = ap + CHUNK
            cur = pltpu.roll(u3[k * TM + a * SUB:k * TM + (a + 1) * SUB, :], 1, axis=0)
            prv = pltpu.roll(p3[r0:r0 + SUB, :], 1, axis=0)
            return jnp.where(sub0, prv, cur)

        bias_k = a_b_dw[k * SUB:(k + 1) * SUB, :]
        for a0 in range(0, CHUNK, TB):
            rows = [tile_in(a0 - HIST + d) for d in range(TB + HIST)]
            accs = [bias_k] * TB
            for j in range(CONV_WIDTH):
                wj = a_w_dw[(j * NBLK + k) * SUB:(j * NBLK + k + 1) * SUB, :]
                for t in range(TB):
                    accs[t] = accs[t] + rows[t + j] * wj
            for t in range(TB):
                c3[k * TM + (a0 + t) * SUB:k * TM + (a0 + t + 1) * SUB, :] = accs[t]

        last = u3[k * TM + (CHUNK - HIST) * SUB:(k + 1) * TM, :]
        tail_ref[k * hrows:(k + 1) * hrows, :] = last
        if not streams_on_sublanes:
            p3[k * hrows:(k + 1) * hrows, :] = last

    gs = slice(2 * D, 3 * D)
    gate = jnp.dot(h, a_w_in[:, gs], preferred_element_type=f32) + a_b_in[:, gs]
    sgate[...] = _silu_bf16(gate.astype(bf16))

    for c in range(NC):
        t0 = c * CHUNK
        cv = jnp.concatenate(
            [c3[pl.ds(k * TM + c, CHUNK, stride=SUB), :] for k in range(NBLK)], axis=1)
        mu = jnp.mean(cv, axis=-1, keepdims=True)
        xc = cv - mu
        ln = xc * lax.rsqrt(jnp.mean(xc * xc, axis=-1, keepdims=True) + EPS)
        ln = (ln * a_ln_g[...] + a_ln_b[...]).astype(bf16)
        cbuf[t0:t0 + CHUNK, :] = _silu_bf16(ln) * sgate[t0:t0 + CHUNK, :]

    y = jnp.dot(cbuf[...], a_w_out[...], preferred_element_type=f32) + a_b_out[...]
    x1 = x + y * _rstd(y) * a_post_g[...]
    x1buf[...] = x1
    xn = x1 * _rstd(x1)
    kv = jnp.dot((xn * kv_g[...]).astype(bf16), w_kv[...], preferred_element_type=f32)
    kv_ref[...] = kv
    k0, k1 = _dup_heads(kv[:, 0:LANES])
    v0, v1 = _dup_heads(kv[:, LANES:2 * LANES])
    for hh, (kk, vv) in enumerate(((k0, v0), (k1, v1))):
        kd[hh, :, WINDOW:WINDOW + seg_rows, :] = kk.astype(bf16).reshape(nseg, seg_rows, LANES)
        vd[hh, :, WINDOW:WINDOW + seg_rows, :] = vv.astype(bf16).reshape(nseg, seg_rows, LANES)
    hb = (xn * b_pre_g[...]).astype(bf16)
    q = jnp.dot(hb, b_w_in[:, 0:D], preferred_element_type=f32) * (HEAD_DIM ** -0.5 * LOG2E)
    lo_mask = lax.broadcasted_iota(jnp.int32, (TM, LANES), 1) < HEAD_DIM
    for blk in range(NBLK):
        qb = q[:, blk * LANES:(blk + 1) * LANES]
        kh, b = divmod(blk, GROUP // 2)
        lo = jnp.where(lo_mask, qb, 0.0).astype(bf16).reshape(NC, CHUNK, LANES)
        hi = jnp.where(lo_mask, 0.0, qb).astype(bf16).reshape(NC, CHUNK, LANES)
        qs[kh, :, b * LANES:b * LANES + CHUNK, :] = lo
        qs[kh, :, b * LANES + CHUNK:(b + 1) * LANES, :] = hi

    zkeys = jnp.zeros((KEYS - WINDOW - CHUNK, LANES), bf16)
    lo64 = lax.broadcasted_iota(jnp.int32, (CHUNK, LANES), 1) < HEAD_DIM

    def attn(c):
        seg, cl = divmod(c, cps)
        row0 = cl * CHUNK
        t0 = c * CHUNK
        if hist_ref is None:
            var = jnp.minimum(step * NC + c + first_bias, 2)
        else:
            var = min(cl + first_bias, 2)
        outs = []
        for kh in range(N_KV):
            kx = jnp.concatenate(
                [kd[kh, seg, row0:row0 + WINDOW + CHUNK, :], zkeys], axis=0)
            vx = jnp.concatenate(
                [vd[kh, seg, row0:row0 + WINDOW + CHUNK, :], zkeys], axis=0)
            s = lax.dot_general(qs[kh, c], kx, (((1,), (1,)), ((), ())),
                                preferred_element_type=f32)
            s = s + bias_ref[var, kh]
            m = jnp.max(s, axis=-1, keepdims=True)
            p = jnp.exp2(s - m)
            l = jnp.sum(p, axis=-1, keepdims=True)
            o = jnp.dot(p.astype(bf16), vx, preferred_element_type=f32)
            o = o * (1.0 / l)
            for b in range(GROUP // 2):
                outs.append(jnp.where(lo64,
                                      o[b * LANES:b * LANES + CHUNK, :],
                                      o[b * LANES + CHUNK:(b + 1) * LANES, :]))
        oraw[t0:t0 + CHUNK, :] = jnp.concatenate(outs, axis=1).astype(bf16)

    for c in range(NC // 2):
        attn(c)
    g2 = jnp.dot(hb, b_w_in[:, D:2 * D], preferred_element_type=f32)
    g2buf[...] = _silu_bf16(g2.astype(bf16))
    for c in range(NC // 2, NC):
        attn(c)

    y2 = jnp.dot(oraw[...] * g2buf[...], b_w_out[...], preferred_element_type=f32)
    y_ref[...] = x1buf[...] + y2 * _rstd(y2) * b_post_g[...]


def _attn_bias(sinks):
    h = jnp.arange(1, N_HEADS + 1, dtype=f32)
    slopes = (2.0 ** (-8.0 * h / N_HEADS)).reshape(N_KV, GROUP, 1, 1)
    i = jnp.arange(CHUNK, dtype=jnp.int32)[:, None]
    j = jnp.arange(KEYS, dtype=jnp.int32)[None, :]
    dist = jnp.abs(WINDOW + i - j).astype(f32)[None, None]
    nkeys = WINDOW + CHUNK
    jj = j[None, None]
    base = jnp.where(jj < nkeys, -(slopes * dist) * LOG2E,
                     jnp.where(jj == nkeys,
                               sinks.astype(f32).reshape(N_KV, GROUP, 1, 1) * LOG2E, NEG_INF))
    variants = [jnp.where(jj < WINDOW - v * CHUNK, NEG_INF, base) for v in range(3)]
    return jnp.stack(variants, axis=0).reshape(3, N_KV, QROWS, KEYS)


def _const_spec(shape):
    nd = len(shape)
    return pl.BlockSpec(shape, lambda i, _nd=nd: (0,) * _nd,
                        pipeline_mode=pl.Buffered(1))


def _trunk(x2d, hist, kpre, vpre, weights, bias, seg_rows, first_bias):
    T = x2d.shape[0]
    nseg = TM // seg_rows
    n_steps = T // TM
    has_state = hist is not None
    hrows = NBLK * HIST * SUB

    in_specs = [pl.BlockSpec((TM, D), lambda i: (i, 0))]
    args = [x2d]
    if has_state:
        in_specs += [pl.BlockSpec((hrows, LANES), lambda i: (i, 0)),
                     pl.BlockSpec((nseg, WINDOW, LANES), lambda i: (i, 0, 0)),
                     pl.BlockSpec((nseg, WINDOW, LANES), lambda i: (i, 0, 0))]
        args += [hist, kpre, vpre]
    w_list = list(weights[:-2]) + [bias] + list(weights[-2:])
    in_specs += [_const_spec(w.shape) for w in w_list]
    args += w_list

    out_shape = (jax.ShapeDtypeStruct((T, D), f32),
                 jax.ShapeDtypeStruct((n_steps * hrows, LANES), f32),
                 jax.ShapeDtypeStruct((T, 2 * LANES), f32))
    out_specs = (pl.BlockSpec((TM, D), lambda i: (i, 0)),
                 pl.BlockSpec((hrows, LANES), lambda i: (i, 0)),
                 pl.BlockSpec((TM, 2 * LANES), lambda i: (i, 0)))
    scratch = [
        pltpu.VMEM((NBLK * TM, LANES), f32),
        pltpu.VMEM((hrows, LANES), f32),
        pltpu.VMEM((NBLK * TM, LANES), f32),
        pltpu.VMEM((TM, D), bf16),
        pltpu.VMEM((TM, D), bf16),
        pltpu.VMEM((TM, D), f32),
        pltpu.VMEM((N_KV, nseg, seg_rows + WINDOW, LANES), bf16),
        pltpu.VMEM((N_KV, nseg, seg_rows + WINDOW, LANES), bf16),
        pltpu.VMEM((N_KV, NC, QROWS, LANES), bf16),
        pltpu.VMEM((TM, D), bf16),
        pltpu.VMEM((TM, D), bf16),
    ]

    if has_state:
        body = functools.partial(_trunk_kernel, seg_rows, first_bias)
    else:
        def body(x_ref, *rest):
            _trunk_kernel(seg_rows, first_bias, x_ref, None, None, None, *rest)

    return pl.pallas_call(
        body,
        grid=(n_steps,),
        in_specs=in_specs,
        out_specs=out_specs,
        out_shape=out_shape,
        scratch_shapes=scratch,
        compiler_params=pltpu.CompilerParams(
            dimension_semantics=("arbitrary",),
            vmem_limit_bytes=VMEM_LIMIT),
        name="trunk_sample" if has_state else "trunk_prompt",
    )(*args)


def _blocked_cols(w):
    lead = w.shape[:-1]
    a = w[..., 0:D].reshape(*lead, NBLK, 1, LANES)
    gl = w[..., D:2 * D].reshape(*lead, NBLK, 1, LANES)
    agl = jnp.concatenate([a, gl], axis=-2).reshape(*lead, 2 * D)
    return jnp.concatenate([agl, w[..., 2 * D:]], axis=-1)


def kernel(x_prompt, x_sample, state_conv, cache_k, cache_v, a_pre_g, a_w_in, a_b_in, a_w_dw, a_b_dw, a_ln_g, a_ln_b, a_w_out, a_b_out, a_post_g, kv_g, w_kv, b_pre_g, b_w_in, b_sinks, b_w_out, b_post_g):
    row = lambda v: v.reshape(1, -1).astype(f32)
    w_dw = jnp.broadcast_to(a_w_dw[0].astype(f32).reshape(CONV_WIDTH, NBLK, 1, LANES),
                            (CONV_WIDTH, NBLK, SUB, LANES)).reshape(CONV_WIDTH * NBLK * SUB, LANES)
    b_dw = jnp.broadcast_to(a_b_dw[0].astype(f32).reshape(NBLK, 1, LANES),
                            (NBLK, SUB, LANES)).reshape(NBLK * SUB, LANES)
    weights = (row(a_pre_g[0]), _blocked_cols(a_w_in[0]).astype(bf16),
               _blocked_cols(row(a_b_in[0])), w_dw, b_dw, row(a_ln_g[0]), row(a_ln_b[0]),
               a_w_out[0].astype(bf16), row(a_b_out[0]), row(a_post_g[0]),
               row(kv_g), w_kv.astype(bf16), row(b_pre_g[0]), b_w_in[0].astype(bf16),
               b_w_out[0].astype(bf16), row(b_post_g[0]))
    bias = _attn_bias(b_sinks[0])

    B, T, _ = x_prompt.shape
    yp, tail_p, kv_p = _trunk(x_prompt.reshape(B * T, D), None, None, None,
                              weights, bias, seg_rows=TM, first_bias=0)
    y_prompt = yp.reshape(B, T, D)
    tp = tail_p[-NBLK * HIST * SUB:].reshape(NBLK, HIST, SUB, LANES)[:, :, SUB - 1, :]
    conv_p = tp.transpose(1, 0, 2).reshape(1, 1, HIST, D)
    k_p = kv_p[-WINDOW:, 0:LANES].reshape(B, WINDOW, N_KV, HEAD_DIM)
    v_p = kv_p[-WINDOW:, LANES:].reshape(B, WINDOW, N_KV, HEAD_DIM)

    SB, ST, _ = x_sample.shape
    hist = state_conv[0].astype(f32).reshape(SB, HIST, NBLK, LANES).transpose(2, 1, 0, 3)
    hist = hist.reshape(NBLK * HIST * SUB, LANES)
    kpre = cache_k.astype(f32).reshape(SB, WINDOW, LANES)
    vpre = cache_v.astype(f32).reshape(SB, WINDOW, LANES)
    ys, tail_s, kv_s = _trunk(x_sample.reshape(SB * ST, D), hist, kpre, vpre,
                              weights, bias, seg_rows=ST, first_bias=2)
    y_sample = ys.reshape(SB, ST, D)
    conv_s = tail_s.reshape(NBLK, HIST, SB, LANES).transpose(2, 1, 0, 3).reshape(1, SB, HIST, D)
    k_new = kv_s[:, 0:LANES].reshape(SB, ST, N_KV, HEAD_DIM)
    v_new = kv_s[:, LANES:].reshape(SB, ST, N_KV, HEAD_DIM)
    k_s = jnp.concatenate([cache_k.astype(f32)[:, ST:], k_new], axis=1)
    v_s = jnp.concatenate([cache_v.astype(f32)[:, ST:], v_new], axis=1)
    return (y_prompt, y_sample, conv_p, conv_s, k_p, v_p, k_s, v_s)
```

```python
import functools
import math

import jax
import jax.numpy as jnp
from jax import lax
from jax.experimental import pallas as pl
from jax.experimental.pallas import tpu as pltpu

D = 1024
CHUNK = 64
CONV_WIDTH = 31
HIST = CONV_WIDTH - 1
N_HEADS = 16
N_KV = 2
GROUP = N_HEADS // N_KV
HEAD_DIM = 64
WINDOW = 128
EPS = 1e-6
NEG_INF = -1e30
LOG2E = math.log2(math.e)

LANES = 128
SUB = 8
NBLK = D // LANES
TM = SUB * CHUNK
NC = TM // CHUNK
KEYS = 256
QROWS = GROUP * CHUNK
TB = 16
VMEM_LIMIT = 56 * 1024 * 1024

f32 = jnp.float32
bf16 = jnp.bfloat16


def _sigmoid(x):
    return 1.0 / (1.0 + jnp.exp(-x))


def _silu_bf16(x):
    hx = x * 0.5
    return hx * jnp.tanh(hx) + hx


def _rstd(x):
    return lax.rsqrt(jnp.mean(x * x, axis=-1, keepdims=True) + EPS)


def _dup_heads(t):
    lane = lax.broadcasted_iota(jnp.int32, t.shape, 1)
    r = pltpu.roll(t, HEAD_DIM, axis=1)
    lo = lane < HEAD_DIM
    return jnp.where(lo, t, r), jnp.where(lo, r, t)


def _trunk_kernel(seg_rows, first_bias,
                  x_ref, hist_ref, kpre_ref, vpre_ref,
                  a_pre_g, a_w_in, a_b_in, a_w_dw, a_b_dw, a_ln_g, a_ln_b,
                  a_w_out, a_b_out, a_post_g, kv_g, w_kv, b_pre_g, b_w_in,
                  bias_ref, b_w_out, b_post_g,
                  y_ref, tail_ref, kv_ref,
                  u3, p3, c3, sgate, cbuf, x1buf, kd, vd, qs, g2buf, oraw):
    nseg = TM // seg_rows
    cps = seg_rows // CHUNK
    step = pl.program_id(0)
    streams_on_sublanes = hist_ref is not None
    hrows = HIST * SUB

    if hist_ref is None:
        @pl.when(step == 0)
        def _():
            p3[...] = jnp.zeros(p3.shape, f32)
            kd[...] = jnp.zeros(kd.shape, bf16)
            vd[...] = jnp.zeros(vd.shape, bf16)

        for h in range(N_KV):
            kd[h, :, 0:WINDOW, :] = kd[h, :, seg_rows:seg_rows + WINDOW, :]
            vd[h, :, 0:WINDOW, :] = vd[h, :, seg_rows:seg_rows + WINDOW, :]
    else:
        k0, k1 = _dup_heads(kpre_ref[...].reshape(nseg * WINDOW, LANES))
        v0, v1 = _dup_heads(vpre_ref[...].reshape(nseg * WINDOW, LANES))
        kd[0, :, 0:WINDOW, :] = k0.astype(bf16).reshape(nseg, WINDOW, LANES)
        kd[1, :, 0:WINDOW, :] = k1.astype(bf16).reshape(nseg, WINDOW, LANES)
        vd[0, :, 0:WINDOW, :] = v0.astype(bf16).reshape(nseg, WINDOW, LANES)
        vd[1, :, 0:WINDOW, :] = v1.astype(bf16).reshape(nseg, WINDOW, LANES)

    x = x_ref[...]
    h = (x * _rstd(x) * a_pre_g[...]).astype(bf16)
    sub0 = lax.broadcasted_iota(jnp.int32, (SUB, LANES), 0) == 0

    for k in range(NBLK):
        ca = slice(LANES * k, LANES * (k + 1))
        cg = slice(D + LANES * k, D + LANES * (k + 1))
        wk = jnp.concatenate([a_w_in[:, ca], a_w_in[:, cg]], axis=1)
        bk = jnp.concatenate([a_b_in[:, ca], a_b_in[:, cg]], axis=1)
        zk = jnp.dot(h, wk, preferred_element_type=f32) + bk
        uk = zk[:, 0:LANES] * _sigmoid(zk[:, LANES:2 * LANES])
        for s in range(SUB):
            u3[pl.ds(k * TM + s, CHUNK, stride=SUB), :] = uk[s * CHUNK:(s + 1) * CHUNK, :]

        def tile_in(ap):
            if ap >= 0:
                return u3[k * TM + ap * SUB:k * TM + (ap + 1) * SUB, :]
            r0 = k * hrows + (ap + HIST) * SUB
            if streams_on_sublanes:
                return hist_ref[r0:r0 + SUB, :]
            a = ap + CHUNK
            cur = pltpu.roll(u3[k * TM + a * SUB:k * TM + (a + 1) * SUB, :], 1, axis=0)
            prv = pltpu.roll(p3[r0:r0 + SUB, :], 1, axis=0)
            return jnp.where(sub0, prv, cur)

        bias_k = a_b_dw[k * SUB:(k + 1) * SUB, :]
        for a0 in range(0, CHUNK, TB):
            rows = [tile_in(a0 - HIST + d) for d in range(TB + HIST)]
            accs = [bias_k] * TB
            for j in range(CONV_WIDTH):
                wj = a_w_dw[(j * NBLK + k) * SUB:(j * NBLK + k + 1) * SUB, :]
                for t in range(TB):
                    accs[t] = accs[t] + rows[t + j] * wj
            for t in range(TB):
                c3[k * TM + (a0 + t) * SUB:k * TM + (a0 + t + 1) * SUB, :] = accs[t]

        last = u3[k * TM + (CHUNK - HIST) * SUB:(k + 1) * TM, :]
        tail_ref[k * hrows:(k + 1) * hrows, :] = last
        if not streams_on_sublanes:
            p3[k * hrows:(k + 1) * hrows, :] = last

    gs = slice(2 * D, 3 * D)
    gate = jnp.dot(h, a_w_in[:, gs], preferred_element_type=f32) + a_b_in[:, gs]
    sgate[...] = _silu_bf16(gate.astype(bf16))

    for c in range(NC):
        t0 = c * CHUNK
        cv = jnp.concatenate(
            [c3[pl.ds(k * TM + c, CHUNK, stride=SUB), :] for k in range(NBLK)], axis=1)
        mu = jnp.mean(cv, axis=-1, keepdims=True)
        xc = cv - mu
        ln = xc * lax.rsqrt(jnp.mean(xc * xc, axis=-1, keepdims=True) + EPS)
        ln = (ln * a_ln_g[...] + a_ln_b[...]).astype(bf16)
        cbuf[t0:t0 + CHUNK, :] = _silu_bf16(ln) * sgate[t0:t0 + CHUNK, :]

    y = jnp.dot(cbuf[...], a_w_out[...], preferred_element_type=f32) + a_b_out[...]
    x1 = x + y * _rstd(y) * a_post_g[...]
    x1buf[...] = x1
    xn = x1 * _rstd(x1)
    kv = jnp.dot((xn * kv_g[...]).astype(bf16), w_kv[...], preferred_element_type=f32)
    kv_ref[...] = kv
    k0, k1 = _dup_heads(kv[:, 0:LANES])
    v0, v1 = _dup_heads(kv[:, LANES:2 * LANES])
    for hh, (kk, vv) in enumerate(((k0, v0), (k1, v1))):
        kd[hh, :, WINDOW:WINDOW + seg_rows, :] = kk.astype(bf16).reshape(nseg, seg_rows, LANES)
        vd[hh, :, WINDOW:WINDOW + seg_rows, :] = vv.astype(bf16).reshape(nseg, seg_rows, LANES)
    hb = (xn * b_pre_g[...]).astype(bf16)
    q = jnp.dot(hb, b_w_in[:, 0:D], preferred_element_type=f32) * (HEAD_DIM ** -0.5 * LOG2E)
    lo_mask = lax.broadcasted_iota(jnp.int32, (TM, LANES), 1) < HEAD_DIM
    for blk in range(NBLK):
        qb = q[:, blk * LANES:(blk + 1) * LANES]
        kh, b = divmod(blk, GROUP // 2)
        lo = jnp.where(lo_mask, qb, 0.0).astype(bf16).reshape(NC, CHUNK, LANES)
        hi = jnp.where(lo_mask, 0.0, qb).astype(bf16).reshape(NC, CHUNK, LANES)
        qs[kh, :, b * LANES:b * LANES + CHUNK, :] = lo
        qs[kh, :, b * LANES + CHUNK:(b + 1) * LANES, :] = hi

    zkeys = jnp.zeros((KEYS - WINDOW - CHUNK, LANES), bf16)
    lo64 = lax.broadcasted_iota(jnp.int32, (CHUNK, LANES), 1) < HEAD_DIM

    def attn(c):
        seg, cl = divmod(c, cps)
        row0 = cl * CHUNK
        t0 = c * CHUNK
        if hist_ref is None:
            var = jnp.minimum(step * NC + c + first_bias, 2)
        else:
            var = min(cl + first_bias, 2)
        outs = []
        for kh in range(N_KV):
            kx = jnp.concatenate(
                [kd[kh, seg, row0:row0 + WINDOW + CHUNK, :], zkeys], axis=0)
            vx = jnp.concatenate(
                [vd[kh, seg, row0:row0 + WINDOW + CHUNK, :], zkeys], axis=0)
            s = lax.dot_general(qs[kh, c], kx, (((1,), (1,)), ((), ())),
                                preferred_element_type=f32)
            s = s + bias_ref[var, kh]
            m = jnp.max(s, axis=-1, keepdims=True)
            p = jnp.exp2(s - m)
            l = jnp.sum(p, axis=-1, keepdims=True)
            o = jnp.dot(p.astype(bf16), vx, preferred_element_type=f32)
            o = o * (1.0 / l)
            for b in range(GROUP // 2):
                outs.append(jnp.where(lo64,
                                      o[b * LANES:b * LANES + CHUNK, :],
                                      o[b * LANES + CHUNK:(b + 1) * LANES, :]))
        oraw[t0:t0 + CHUNK, :] = jnp.concatenate(outs, axis=1).astype(bf16)

    for c in range(NC // 2):
        attn(c)
    g2 = jnp.dot(hb, b_w_in[:, D:2 * D], preferred_element_type=f32)
    g2buf[...] = _silu_bf16(g2.astype(bf16))
    for c in range(NC // 2, NC):
        attn(c)

    y2 = jnp.dot(oraw[...] * g2buf[...], b_w_out[...], preferred_element_type=f32)
    y_ref[...] = x1buf[...] + y2 * _rstd(y2) * b_post_g[...]


def _attn_bias(sinks):
    h = jnp.arange(1, N_HEADS + 1, dtype=f32)
    slopes = (2.0 ** (-8.0 * h / N_HEADS)).reshape(N_KV, GROUP, 1, 1)
    i = jnp.arange(CHUNK, dtype=jnp.int32)[:, None]
    j = jnp.arange(KEYS, dtype=jnp.int32)[None, :]
    dist = jnp.abs(WINDOW + i - j).astype(f32)[None, None]
    nkeys = WINDOW + CHUNK
    jj = j[None, None]
    base = jnp.where(jj < nkeys, -(slopes * dist) * LOG2E,
                     jnp.where(jj == nkeys,
                               sinks.astype(f32).reshape(N_KV, GROUP, 1, 1) * LOG2E, NEG_INF))
    variants = [jnp.where(jj < WINDOW - v * CHUNK, NEG_INF, base) for v in range(3)]
    return jnp.stack(variants, axis=0).reshape(3, N_KV, QROWS, KEYS)


def _const_spec(shape):
    nd = len(shape)
    return pl.BlockSpec(shape, lambda i, _nd=nd: (0,) * _nd,
                        pipeline_mode=pl.Buffered(1))


def _trunk(x2d, hist, kpre, vpre, weights, bias, seg_rows, first_bias):
    T = x2d.shape[0]
    nseg = TM // seg_rows
    n_steps = T // TM
    has_state = hist is not None
    hrows = NBLK * HIST * SUB

    in_specs = [pl.BlockSpec((TM, D), lambda i: (i, 0))]
    args = [x2d]
    if has_state:
        in_specs += [pl.BlockSpec((hrows, LANES), lambda i: (i, 0)),
                     pl.BlockSpec((nseg, WINDOW, LANES), lambda i: (i, 0, 0)),
                     pl.BlockSpec((nseg, WINDOW, LANES), lambda i: (i, 0, 0))]
        args += [hist, kpre, vpre]
    w_list = list(weights[:-2]) + [bias] + list(weights[-2:])
    in_specs += [_const_spec(w.shape) for w in w_list]
    args += w_list

    out_shape = (jax.ShapeDtypeStruct((T, D), f32),
                 jax.ShapeDtypeStruct((n_steps * hrows, LANES), f32),
                 jax.ShapeDtypeStruct((T, 2 * LANES), f32))
    out_specs = (pl.BlockSpec((TM, D), lambda i: (i, 0)),
                 pl.BlockSpec((hrows, LANES), lambda i: (i, 0)),
                 pl.BlockSpec((TM, 2 * LANES), lambda i: (i, 0)))
    scratch = [
        pltpu.VMEM((NBLK * TM, LANES), f32),
        pltpu.VMEM((hrows, LANES), f32),
        pltpu.VMEM((NBLK * TM, LANES), f32),
        pltpu.VMEM((TM, D), bf16),
        pltpu.VMEM((TM, D), bf16),
        pltpu.VMEM((TM, D), f32),
        pltpu.VMEM((N_KV, nseg, seg_rows + WINDOW, LANES), bf16),
        pltpu.VMEM((N_KV, nseg, seg_rows + WINDOW, LANES), bf16),
        pltpu.VMEM((N_KV, NC, QROWS, LANES), bf16),
        pltpu.VMEM((TM, D), bf16),
        pltpu.VMEM((TM, D), bf16),
    ]

    if has_state:
        body = functools.partial(_trunk_kernel, seg_rows, first_bias)
    else:
        def body(x_ref, *rest):
            _trunk_kernel(seg_rows, first_bias, x_ref, None, None, None, *rest)

    return pl.pallas_call(
        body,
        grid=(n_steps,),
        in_specs=in_specs,
        out_specs=out_specs,
        out_shape=out_shape,
        scratch_shapes=scratch,
        compiler_params=pltpu.CompilerParams(
            dimension_semantics=("arbitrary",),
            vmem_limit_bytes=VMEM_LIMIT),
        name="trunk_sample" if has_state else "trunk_prompt",
    )(*args)


def kernel(x_prompt, x_sample, state_conv, cache_k, cache_v, a_pre_g, a_w_in, a_b_in, a_w_dw, a_b_dw, a_ln_g, a_ln_b, a_w_out, a_b_out, a_post_g, kv_g, w_kv, b_pre_g, b_w_in, b_sinks, b_w_out, b_post_g):
    row = lambda v: v.reshape(1, -1).astype(f32)
    w_dw = jnp.broadcast_to(a_w_dw[0].astype(f32).reshape(CONV_WIDTH, NBLK, 1, LANES),
                            (CONV_WIDTH, NBLK, SUB, LANES)).reshape(CONV_WIDTH * NBLK * SUB, LANES)
    b_dw = jnp.broadcast_to(a_b_dw[0].astype(f32).reshape(NBLK, 1, LANES),
                            (NBLK, SUB, LANES)).reshape(NBLK * SUB, LANES)
    weights = (row(a_pre_g[0]), a_w_in[0].astype(bf16), row(a_b_in[0]),
               w_dw, b_dw, row(a_ln_g[0]), row(a_ln_b[0]),
               a_w_out[0].astype(bf16), row(a_b_out[0]), row(a_post_g[0]),
               row(kv_g), w_kv.astype(bf16), row(b_pre_g[0]), b_w_in[0].astype(bf16),
               b_w_out[0].astype(bf16), row(b_post_g[0]))
    bias = _attn_bias(b_sinks[0])

    B, T, _ = x_prompt.shape
    yp, tail_p, kv_p = _trunk(x_prompt.reshape(B * T, D), None, None, None,
                              weights, bias, seg_rows=TM, first_bias=0)
    y_prompt = yp.reshape(B, T, D)
    tp = tail_p[-NBLK * HIST * SUB:].reshape(NBLK, HIST, SUB, LANES)[:, :, SUB - 1, :]
    conv_p = tp.transpose(1, 0, 2).reshape(1, 1, HIST, D)
    k_p = kv_p[-WINDOW:, 0:LANES].reshape(B, WINDOW, N_KV, HEAD_DIM)
    v_p = kv_p[-WINDOW:, LANES:].reshape(B, WINDOW, N_KV, HEAD_DIM)

    SB, ST, _ = x_sample.shape
    hist = state_conv[0].astype(f32).reshape(SB, HIST, NBLK, LANES).transpose(2, 1, 0, 3)
    hist = hist.reshape(NBLK * HIST * SUB, LANES)
    kpre = cache_k.astype(f32).reshape(SB, WINDOW, LANES)
    vpre = cache_v.astype(f32).reshape(SB, WINDOW, LANES)
    ys, tail_s, kv_s = _trunk(x_sample.reshape(SB * ST, D), hist, kpre, vpre,
                              weights, bias, seg_rows=ST, first_bias=2)
    y_sample = ys.reshape(SB, ST, D)
    conv_s = tail_s.reshape(NBLK, HIST, SB, LANES).transpose(2, 1, 0, 3).reshape(1, SB, HIST, D)
    k_new = kv_s[:, 0:LANES].reshape(SB, ST, N_KV, HEAD_DIM)
    v_new = kv_s[:, LANES:].reshape(SB, ST, N_KV, HEAD_DIM)
    k_s = jnp.concatenate([cache_k.astype(f32)[:, ST:], k_new], axis=1)
    v_s = jnp.concatenate([cache_v.astype(f32)[:, ST:], v_new], axis=1)
    return (y_prompt, y_sample, conv_p, conv_s, k_p, v_p, k_s, v_s)
```

```python
import functools
import math

import jax
import jax.numpy as jnp
from jax import lax
from jax.experimental import pallas as pl
from jax.experimental.pallas import tpu as pltpu

D = 1024
CHUNK = 64
CONV_WIDTH = 31
HIST = CONV_WIDTH - 1
N_HEADS = 16
N_KV = 2
GROUP = N_HEADS // N_KV
HEAD_DIM = 64
WINDOW = 128
EPS = 1e-6
NEG_INF = -1e30
LOG2E = math.log2(math.e)

LANES = 128
SUB = 8
NBLK = D // LANES
TM = SUB * CHUNK
NC = TM // CHUNK
KEYS = 256
QROWS = GROUP * CHUNK
TB = 16
VMEM_LIMIT = 58 * 1024 * 1024

WOVEN = ("F0 F1 F2 C0 F3 F4 C1 F5 F6 C2 F7 F8 C3 W C4 KV Q C5 G "
         "A0 A1 C6 A2 A3 Fg C7 A4 A5 A6 A7 O L0 L1 L2 L3 L4 L5 L6 L7").split()
PINS = {0: 5, 2: 6, 4: 7}
PLAIN = (["F0"] + [p for k in range(NBLK) for p in (f"F{k + 1}", f"C{k}")]
         + ["Fg"] + [f"L{c}" for c in range(NC)] + ["W", "KV", "Q"]
         + [f"A{c}" for c in range(NC // 2)] + ["G"]
         + [f"A{c}" for c in range(NC // 2, NC)] + ["O"])

f32 = jnp.float32
bf16 = jnp.bfloat16


def _sigmoid(x):
    return 1.0 / (1.0 + jnp.exp(-x))


def _silu_bf16(x):
    hx = x * 0.5
    return hx * jnp.tanh(hx) + hx


def _rstd(x):
    return lax.rsqrt(jnp.mean(x * x, axis=-1, keepdims=True) + EPS)


def _dup_heads(t):
    lane = lax.broadcasted_iota(jnp.int32, t.shape, 1)
    r = pltpu.roll(t, HEAD_DIM, axis=1)
    lo = lane < HEAD_DIM
    return jnp.where(lo, t, r), jnp.where(lo, r, t)


def _trunk_kernel(seg_rows, first_bias, skew,
                  x_ref, hist_ref, kpre_ref, vpre_ref,
                  a_pre_g, a_w_in, a_b_in, a_w_dw, a_b_dw, a_ln_g, a_ln_b,
                  a_w_out, a_b_out, a_post_g, kv_g, w_kv, b_pre_g, b_w_in,
                  bias_ref, b_w_out, b_post_g,
                  y_ref, tail_ref, kv_ref,
                  u3, p3, c3, sgate, cbuf, hbuf, hkv, hb, x1buf, kd, vd, qs, g2buf, oraw,
                  *lagged):
    nseg = TM // seg_rows
    cps = seg_rows // CHUNK
    step = pl.program_id(0)
    tile = step - 1 if skew else step
    streams_on_sublanes = hist_ref is not None
    hrows = HIST * SUB
    if skew:
        csrc, xsrc = lagged
    else:
        csrc, xsrc = cbuf, x_ref
    conv_done = {}

    if hist_ref is None:
        @pl.when(step == 0)
        def _():
            p3[...] = jnp.zeros(p3.shape, f32)
            kd[...] = jnp.zeros(kd.shape, bf16)
            vd[...] = jnp.zeros(vd.shape, bf16)
            if skew:
                cbuf[...] = jnp.zeros(cbuf.shape, bf16)
                xsrc[...] = jnp.zeros(xsrc.shape, f32)

        if skew:
            csrc[...] = cbuf[...]
        for h in range(N_KV):
            kd[h, :, 0:WINDOW, :] = kd[h, :, seg_rows:seg_rows + WINDOW, :]
            vd[h, :, 0:WINDOW, :] = vd[h, :, seg_rows:seg_rows + WINDOW, :]
    else:
        k0, k1 = _dup_heads(kpre_ref[...].reshape(nseg * WINDOW, LANES))
        v0, v1 = _dup_heads(vpre_ref[...].reshape(nseg * WINDOW, LANES))
        kd[0, :, 0:WINDOW, :] = k0.astype(bf16).reshape(nseg, WINDOW, LANES)
        kd[1, :, 0:WINDOW, :] = k1.astype(bf16).reshape(nseg, WINDOW, LANES)
        vd[0, :, 0:WINDOW, :] = v0.astype(bf16).reshape(nseg, WINDOW, LANES)
        vd[1, :, 0:WINDOW, :] = v1.astype(bf16).reshape(nseg, WINDOW, LANES)

    def f_norm():
        x = x_ref[...]
        hbuf[...] = (x * _rstd(x) * a_pre_g[...]).astype(bf16)

    def f_proj(k):
        ca = slice(LANES * k, LANES * (k + 1))
        cg = slice(D + LANES * k, D + LANES * (k + 1))
        wk = jnp.concatenate([a_w_in[:, ca], a_w_in[:, cg]], axis=1)
        bk = jnp.concatenate([a_b_in[:, ca], a_b_in[:, cg]], axis=1)
        zk = jnp.dot(hbuf[...], wk, preferred_element_type=f32) + bk
        uk = zk[:, 0:LANES] * _sigmoid(zk[:, LANES:2 * LANES])
        for s in range(SUB):
            u3[pl.ds(k * TM + s, CHUNK, stride=SUB), :] = uk[s * CHUNK:(s + 1) * CHUNK, :]

    def f_conv(k):
        sub0 = lax.broadcasted_iota(jnp.int32, (SUB, LANES), 0) == 0

        def tile_in(ap):
            if ap >= 0:
                return u3[k * TM + ap * SUB:k * TM + (ap + 1) * SUB, :]
            r0 = k * hrows + (ap + HIST) * SUB
            if streams_on_sublanes:
                return hist_ref[r0:r0 + SUB, :]
            a = ap + CHUNK
            cur = pltpu.roll(u3[k * TM + a * SUB:k * TM + (a + 1) * SUB, :], 1, axis=0)
            prv = pltpu.roll(p3[r0:r0 + SUB, :], 1, axis=0)
            return jnp.where(sub0, prv, cur)

        bias_k = a_b_dw[k * SUB:(k + 1) * SUB, :]
        for a0 in range(0, CHUNK, TB):
            rows = [tile_in(a0 - HIST + d) for d in range(TB + HIST)]
            accs = [bias_k] * TB
            for j in range(CONV_WIDTH):
                wj = a_w_dw[(j * NBLK + k) * SUB:(j * NBLK + k + 1) * SUB, :]
                for t in range(TB):
                    accs[t] = accs[t] + rows[t + j] * wj
            for t in range(TB):
                c3[k * TM + (a0 + t) * SUB:k * TM + (a0 + t + 1) * SUB, :] = accs[t]
        conv_done[k] = jnp.minimum(jnp.abs(accs[TB - 1]), 0.0)

        last = u3[k * TM + (CHUNK - HIST) * SUB:(k + 1) * TM, :]
        tail_ref[k * hrows:(k + 1) * hrows, :] = last
        if not streams_on_sublanes:
            p3[k * hrows:(k + 1) * hrows, :] = last

    def f_gate():
        gs = slice(2 * D, 3 * D)
        gate = jnp.dot(hbuf[...], a_w_in[:, gs], preferred_element_type=f32) + a_b_in[:, gs]
        sgate[...] = _silu_bf16(gate.astype(bf16))

    def f_ln(c):
        t0 = c * CHUNK
        cv = jnp.concatenate(
            [c3[pl.ds(k * TM + c, CHUNK, stride=SUB), :] for k in range(NBLK)], axis=1)
        mu = jnp.mean(cv, axis=-1, keepdims=True)
        xc = cv - mu
        ln = xc * lax.rsqrt(jnp.mean(xc * xc, axis=-1, keepdims=True) + EPS)
        ln = (ln * a_ln_g[...] + a_ln_b[...]).astype(bf16)
        cbuf[t0:t0 + CHUNK, :] = _silu_bf16(ln) * sgate[t0:t0 + CHUNK, :]

    def b_out_a():
        y = jnp.dot(csrc[...], a_w_out[...], preferred_element_type=f32) + a_b_out[...]
        x1 = xsrc[...] + y * _rstd(y) * a_post_g[...]
        x1buf[...] = x1
        xn = x1 * _rstd(x1)
        hkv[...] = (xn * kv_g[...]).astype(bf16)
        hb[...] = (xn * b_pre_g[...]).astype(bf16)

    def b_kv():
        kv = jnp.dot(hkv[...], w_kv[...], preferred_element_type=f32)
        kv_ref[...] = kv
        k0, k1 = _dup_heads(kv[:, 0:LANES])
        v0, v1 = _dup_heads(kv[:, LANES:2 * LANES])
        for hh, (kk, vv) in enumerate(((k0, v0), (k1, v1))):
            kd[hh, :, WINDOW:WINDOW + seg_rows, :] = kk.astype(bf16).reshape(nseg, seg_rows, LANES)
            vd[hh, :, WINDOW:WINDOW + seg_rows, :] = vv.astype(bf16).reshape(nseg, seg_rows, LANES)

    def b_q():
        q = jnp.dot(hb[...], b_w_in[:, 0:D], preferred_element_type=f32) * (HEAD_DIM ** -0.5 * LOG2E)
        lo_mask = lax.broadcasted_iota(jnp.int32, (TM, LANES), 1) < HEAD_DIM
        for blk in range(NBLK):
            qb = q[:, blk * LANES:(blk + 1) * LANES]
            kh, b = divmod(blk, GROUP // 2)
            lo = jnp.where(lo_mask, qb, 0.0).astype(bf16).reshape(NC, CHUNK, LANES)
            hi = jnp.where(lo_mask, 0.0, qb).astype(bf16).reshape(NC, CHUNK, LANES)
            qs[kh, :, b * LANES:b * LANES + CHUNK, :] = lo
            qs[kh, :, b * LANES + CHUNK:(b + 1) * LANES, :] = hi

    def b_gate():
        g2 = jnp.dot(hb[...], b_w_in[:, D:2 * D], preferred_element_type=f32)
        g2buf[...] = _silu_bf16(g2.astype(bf16))

    def b_attn(c):
        zkeys = jnp.zeros((KEYS - WINDOW - CHUNK, LANES), bf16)
        lo64 = lax.broadcasted_iota(jnp.int32, (CHUNK, LANES), 1) < HEAD_DIM
        seg, cl = divmod(c, cps)
        row0 = cl * CHUNK
        t0 = c * CHUNK
        if hist_ref is None:
            var = jnp.clip(tile * NC + c + first_bias, 0, 2)
        else:
            var = min(cl + first_bias, 2)
        pin = conv_done.get(PINS.get(c)) if skew else None
        outs = []
        for kh in range(N_KV):
            keys = kd[kh, seg, row0:row0 + WINDOW + CHUNK, :]
            if pin is not None:
                z16 = jnp.concatenate([pin, pin], axis=0).astype(bf16)
                keys = keys + jnp.concatenate([z16] * ((WINDOW + CHUNK) // (2 * SUB)), axis=0)
            kx = jnp.concatenate([keys, zkeys], axis=0)
            vx = jnp.concatenate(
                [vd[kh, seg, row0:row0 + WINDOW + CHUNK, :], zkeys], axis=0)
            s = lax.dot_general(qs[kh, c], kx, (((1,), (1,)), ((), ())),
                                preferred_element_type=f32)
            s = s + bias_ref[var, kh]
            m = jnp.max(s, axis=-1, keepdims=True)
            p = jnp.exp2(s - m)
            l = jnp.sum(p, axis=-1, keepdims=True)
            o = jnp.dot(p.astype(bf16), vx, preferred_element_type=f32)
            o = o * (1.0 / l)
            for b in range(GROUP // 2):
                outs.append(jnp.where(lo64,
                                      o[b * LANES:b * LANES + CHUNK, :],
                                      o[b * LANES + CHUNK:(b + 1) * LANES, :]))
        oraw[t0:t0 + CHUNK, :] = jnp.concatenate(outs, axis=1).astype(bf16)

    def b_out_b():
        y2 = jnp.dot(oraw[...] * g2buf[...], b_w_out[...], preferred_element_type=f32)
        y_ref[...] = x1buf[...] + y2 * _rstd(y2) * b_post_g[...]

    pieces = {"F0": f_norm, "Fg": f_gate, "W": b_out_a, "KV": b_kv, "Q": b_q,
              "G": b_gate, "O": b_out_b}
    for k in range(NBLK):
        pieces[f"F{k + 1}"] = functools.partial(f_proj, k)
        pieces[f"C{k}"] = functools.partial(f_conv, k)
    for c in range(NC):
        pieces[f"L{c}"] = functools.partial(f_ln, c)
        pieces[f"A{c}"] = functools.partial(b_attn, c)

    order = WOVEN if skew else PLAIN
    assert sorted(order) == sorted(pieces)
    for name in order:
        pieces[name]()
    if skew:
        xsrc[...] = x_ref[...]


def _attn_bias(sinks):
    h = jnp.arange(1, N_HEADS + 1, dtype=f32)
    slopes = (2.0 ** (-8.0 * h / N_HEADS)).reshape(N_KV, GROUP, 1, 1)
    i = jnp.arange(CHUNK, dtype=jnp.int32)[:, None]
    j = jnp.arange(KEYS, dtype=jnp.int32)[None, :]
    dist = jnp.abs(WINDOW + i - j).astype(f32)[None, None]
    nkeys = WINDOW + CHUNK
    jj = j[None, None]
    base = jnp.where(jj < nkeys, -(slopes * dist) * LOG2E,
                     jnp.where(jj == nkeys,
                               sinks.astype(f32).reshape(N_KV, GROUP, 1, 1) * LOG2E, NEG_INF))
    variants = [jnp.where(jj < WINDOW - v * CHUNK, NEG_INF, base) for v in range(3)]
    return jnp.stack(variants, axis=0).reshape(3, N_KV, QROWS, KEYS)


def _const_spec(shape):
    nd = len(shape)
    return pl.BlockSpec(shape, lambda i, _nd=nd: (0,) * _nd,
                        pipeline_mode=pl.Buffered(1))


def _trunk(x2d, hist, kpre, vpre, weights, bias, seg_rows, first_bias):
    T = x2d.shape[0]
    nseg = TM // seg_rows
    n_tiles = T // TM
    has_state = hist is not None
    skew = not has_state and n_tiles > 1
    n_steps = n_tiles + 1 if skew else n_tiles
    hrows = NBLK * HIST * SUB

    if skew:
        front_tile = lambda i: jnp.minimum(i, n_tiles - 1)
        back_tile = lambda i: jnp.maximum(i - 1, 0)
    else:
        front_tile = back_tile = lambda i: i

    in_specs = [pl.BlockSpec((TM, D), lambda i: (front_tile(i), 0))]
    args = [x2d]
    if has_state:
        in_specs += [pl.BlockSpec((hrows, LANES), lambda i: (i, 0)),
                     pl.BlockSpec((nseg, WINDOW, LANES), lambda i: (i, 0, 0)),
                     pl.BlockSpec((nseg, WINDOW, LANES), lambda i: (i, 0, 0))]
        args += [hist, kpre, vpre]
    w_list = list(weights[:-2]) + [bias] + list(weights[-2:])
    in_specs += [_const_spec(w.shape) for w in w_list]
    args += w_list

    out_shape = (jax.ShapeDtypeStruct((T, D), f32),
                 jax.ShapeDtypeStruct((n_tiles * hrows, LANES), f32),
                 jax.ShapeDtypeStruct((T, 2 * LANES), f32))
    out_specs = (pl.BlockSpec((TM, D), lambda i: (back_tile(i), 0)),
                 pl.BlockSpec((hrows, LANES), lambda i: (front_tile(i), 0)),
                 pl.BlockSpec((TM, 2 * LANES), lambda i: (back_tile(i), 0)))
    scratch = [
        pltpu.VMEM((NBLK * TM, LANES), f32),
        pltpu.VMEM((hrows, LANES), f32),
        pltpu.VMEM((NBLK * TM, LANES), f32),
        pltpu.VMEM((TM, D), bf16),
        pltpu.VMEM((TM, D), bf16),
        pltpu.VMEM((TM, D), bf16),
        pltpu.VMEM((TM, D), bf16),
        pltpu.VMEM((TM, D), bf16),
        pltpu.VMEM((TM, D), f32),
        pltpu.VMEM((N_KV, nseg, seg_rows + WINDOW, LANES), bf16),
        pltpu.VMEM((N_KV, nseg, seg_rows + WINDOW, LANES), bf16),
        pltpu.VMEM((N_KV, NC, QROWS, LANES), bf16),
        pltpu.VMEM((TM, D), bf16),
        pltpu.VMEM((TM, D), bf16),
    ]
    if skew:
        scratch += [pltpu.VMEM((TM, D), bf16),
                    pltpu.VMEM((TM, D), f32)]

    if has_state:
        body = functools.partial(_trunk_kernel, seg_rows, first_bias, skew)
    else:
        def body(x_ref, *rest):
            _trunk_kernel(seg_rows, first_bias, skew, x_ref, None, None, None, *rest)

    return pl.pallas_call(
        body,
        grid=(n_steps,),
        in_specs=in_specs,
        out_specs=out_specs,
        out_shape=out_shape,
        scratch_shapes=scratch,
        compiler_params=pltpu.CompilerParams(
            dimension_semantics=("arbitrary",),
            vmem_limit_bytes=VMEM_LIMIT),
        name="trunk_sample" if has_state else "trunk_prompt",
    )(*args)


def kernel(x_prompt, x_sample, state_conv, cache_k, cache_v, a_pre_g, a_w_in, a_b_in, a_w_dw, a_b_dw, a_ln_g, a_ln_b, a_w_out, a_b_out, a_post_g, kv_g, w_kv, b_pre_g, b_w_in, b_sinks, b_w_out, b_post_g):
    row = lambda v: v.reshape(1, -1).astype(f32)
    w_dw = jnp.broadcast_to(a_w_dw[0].astype(f32).reshape(CONV_WIDTH, NBLK, 1, LANES),
                            (CONV_WIDTH, NBLK, SUB, LANES)).reshape(CONV_WIDTH * NBLK * SUB, LANES)
    b_dw = jnp.broadcast_to(a_b_dw[0].astype(f32).reshape(NBLK, 1, LANES),
                            (NBLK, SUB, LANES)).reshape(NBLK * SUB, LANES)
    weights = (row(a_pre_g[0]), a_w_in[0].astype(bf16), row(a_b_in[0]),
               w_dw, b_dw, row(a_ln_g[0]), row(a_ln_b[0]),
               a_w_out[0].astype(bf16), row(a_b_out[0]), row(a_post_g[0]),
               row(kv_g), w_kv.astype(bf16), row(b_pre_g[0]), b_w_in[0].astype(bf16),
               b_w_out[0].astype(bf16), row(b_post_g[0]))
    bias = _attn_bias(b_sinks[0])

    B, T, _ = x_prompt.shape
    yp, tail_p, kv_p = _trunk(x_prompt.reshape(B * T, D), None, None, None,
                              weights, bias, seg_rows=TM, first_bias=0)
    y_prompt = yp.reshape(B, T, D)
    tp = tail_p[-NBLK * HIST * SUB:].reshape(NBLK, HIST, SUB, LANES)[:, :, SUB - 1, :]
    conv_p = tp.transpose(1, 0, 2).reshape(1, 1, HIST, D)
    k_p = kv_p[-WINDOW:, 0:LANES].reshape(B, WINDOW, N_KV, HEAD_DIM)
    v_p = kv_p[-WINDOW:, LANES:].reshape(B, WINDOW, N_KV, HEAD_DIM)

    SB, ST, _ = x_sample.shape
    hist = state_conv[0].astype(f32).reshape(SB, HIST, NBLK, LANES).transpose(2, 1, 0, 3)
    hist = hist.reshape(NBLK * HIST * SUB, LANES)
    kpre = cache_k.astype(f32).reshape(SB, WINDOW, LANES)
    vpre = cache_v.astype(f32).reshape(SB, WINDOW, LANES)
    ys, tail_s, kv_s = _trunk(x_sample.reshape(SB * ST, D), hist, kpre, vpre,
                              weights, bias, seg_rows=ST, first_bias=2)
    y_sample = ys.reshape(SB, ST, D)
    conv_s = tail_s.reshape(NBLK, HIST, SB, LANES).transpose(2, 1, 0, 3).reshape(1, SB, HIST, D)
    k_new = kv_s[:, 0:LANES].reshape(SB, ST, N_KV, HEAD_DIM)
    v_new = kv_s[:, LANES:].reshape(SB, ST, N_KV, HEAD_DIM)
    k_s = jnp.concatenate([cache_k.astype(f32)[:, ST:], k_new], axis=1)
    v_s = jnp.concatenate([cache_v.astype(f32)[:, ST:], v_new], axis=1)
    return (y_prompt, y_sample, conv_p, conv_s, k_p, v_p, k_s, v_s)
```

```python
import functools
import math

import jax
import jax.numpy as jnp
from jax import lax
from jax.experimental import pallas as pl
from jax.experimental.pallas import tpu as pltpu

D = 1024
CHUNK = 64
CONV_WIDTH = 31
HIST = CONV_WIDTH - 1
N_HEADS = 16
N_KV = 2
GROUP = N_HEADS // N_KV
HEAD_DIM = 64
WINDOW = 128
EPS = 1e-6
NEG_INF = -1e30
LOG2E = math.log2(math.e)

LANES = 128
SUB = 8
NBLK = D // LANES
TM = SUB * CHUNK
NC = TM // CHUNK
KEYS = 256
QROWS = GROUP * CHUNK
TB = 16
VMEM_LIMIT = 58 * 1024 * 1024

WOVEN = ("F0 F1 F2 C0 F3 F4 C1 F5 F6 C2 F7 F8 C3 W C4 KV Q C5 G "
         "A0 A1 C6 A2 A3 Fg C7 A4 A5 A6 A7 O L0 L1 L2 L3 L4 L5 L6 L7").split()
PINS = {0: 5, 2: 6, 4: 7}
PLAIN = (["F0"] + [p for k in range(NBLK) for p in (f"F{k + 1}", f"C{k}")]
         + ["Fg"] + [f"L{c}" for c in range(NC)] + ["W", "KV", "Q"]
         + [f"A{c}" for c in range(NC // 2)] + ["G"]
         + [f"A{c}" for c in range(NC // 2, NC)] + ["O"])

f32 = jnp.float32
bf16 = jnp.bfloat16


def _sigmoid(x):
    return 1.0 / (1.0 + jnp.exp(-x))


def _silu_bf16(x):
    hx = x * 0.5
    return hx * jnp.tanh(hx) + hx


def _rstd(x):
    return lax.rsqrt(jnp.mean(x * x, axis=-1, keepdims=True) + EPS)


def _dup_heads(t):
    lane = lax.broadcasted_iota(jnp.int32, t.shape, 1)
    r = pltpu.roll(t, HEAD_DIM, axis=1)
    lo = lane < HEAD_DIM
    return jnp.where(lo, t, r), jnp.where(lo, r, t)


def _trunk_kernel(seg_rows, first_bias, skew,
                  x_ref, hist_ref, kpre_ref, vpre_ref,
                  a_pre_g, a_w_in, a_b_in, a_w_dw, a_b_dw, a_ln_g, a_ln_b,
                  a_w_out, a_b_out, a_post_g, kv_g, w_kv, b_pre_g, b_w_in,
                  bias_ref, b_w_out, b_post_g,
                  y_ref, tail_ref, kv_ref,
                  u3, p3, c3, sgate, cbuf, hbuf, hkv, hb, x1buf, kd, vd, qs, g2buf, oraw,
                  *lagged):
    nseg = TM // seg_rows
    cps = seg_rows // CHUNK
    step = pl.program_id(0)
    tile = step - 1 if skew else step
    streams_on_sublanes = hist_ref is not None
    hrows = HIST * SUB
    if skew:
        csrc, xsrc = lagged
    else:
        csrc, xsrc = cbuf, x_ref
    conv_done = {}

    if hist_ref is None:
        @pl.when(step == 0)
        def _():
            p3[...] = jnp.zeros(p3.shape, f32)
            kd[...] = jnp.zeros(kd.shape, bf16)
            vd[...] = jnp.zeros(vd.shape, bf16)
            if skew:
                cbuf[...] = jnp.zeros(cbuf.shape, bf16)
                xsrc[...] = jnp.zeros(xsrc.shape, f32)

        if skew:
            csrc[...] = cbuf[...]
        for h in range(N_KV):
            kd[h, :, 0:WINDOW, :] = kd[h, :, seg_rows:seg_rows + WINDOW, :]
            vd[h, :, 0:WINDOW, :] = vd[h, :, seg_rows:seg_rows + WINDOW, :]
    else:
        k0, k1 = _dup_heads(kpre_ref[...].reshape(nseg * WINDOW, LANES))
        v0, v1 = _dup_heads(vpre_ref[...].reshape(nseg * WINDOW, LANES))
        kd[0, :, 0:WINDOW, :] = k0.astype(bf16).reshape(nseg, WINDOW, LANES)
        kd[1, :, 0:WINDOW, :] = k1.astype(bf16).reshape(nseg, WINDOW, LANES)
        vd[0, :, 0:WINDOW, :] = v0.astype(bf16).reshape(nseg, WINDOW, LANES)
        vd[1, :, 0:WINDOW, :] = v1.astype(bf16).reshape(nseg, WINDOW, LANES)

    def f_norm():
        x = x_ref[...]
        hbuf[...] = (x * _rstd(x) * a_pre_g[...]).astype(bf16)

    def f_proj(k):
        ca = slice(LANES * k, LANES * (k + 1))
        cg = slice(D + LANES * k, D + LANES * (k + 1))
        wk = jnp.concatenate([a_w_in[:, ca], a_w_in[:, cg]], axis=1)
        bk = jnp.concatenate([a_b_in[:, ca], a_b_in[:, cg]], axis=1)
        zk = jnp.dot(hbuf[...], wk, preferred_element_type=f32) + bk
        uk = zk[:, 0:LANES] * _sigmoid(zk[:, LANES:2 * LANES])
        for s in range(SUB):
            u3[pl.ds(k * TM + s, CHUNK, stride=SUB), :] = uk[s * CHUNK:(s + 1) * CHUNK, :]

    def f_conv(k):
        sub0 = lax.broadcasted_iota(jnp.int32, (SUB, LANES), 0) == 0

        def tile_in(ap):
            if ap >= 0:
                return u3[k * TM + ap * SUB:k * TM + (ap + 1) * SUB, :]
            r0 = k * hrows + (ap + HIST) * SUB
            if streams_on_sublanes:
                return hist_ref[r0:r0 + SUB, :]
            a = ap + CHUNK
            cur = pltpu.roll(u3[k * TM + a * SUB:k * TM + (a + 1) * SUB, :], 1, axis=0)
            prv = pltpu.roll(p3[r0:r0 + SUB, :], 1, axis=0)
            return jnp.where(sub0, prv, cur)

        bias_k = a_b_dw[k * SUB:(k + 1) * SUB, :]
        for a0 in range(0, CHUNK, TB):
            rows = [tile_in(a0 - HIST + d) for d in range(TB + HIST)]
            accs = [bias_k] * TB
            for j in range(CONV_WIDTH):
                wj = a_w_dw[(j * NBLK + k) * SUB:(j * NBLK + k + 1) * SUB, :]
                for t in range(TB):
                    accs[t] = accs[t] + rows[t + j] * wj
            for t in range(TB):
                c3[k * TM + (a0 + t) * SUB:k * TM + (a0 + t + 1) * SUB, :] = accs[t]
        conv_done[k] = jnp.minimum(jnp.abs(accs[TB - 1]), 0.0)

        last = u3[k * TM + (CHUNK - HIST) * SUB:(k + 1) * TM, :]
        tail_ref[k * hrows:(k + 1) * hrows, :] = last
        if not streams_on_sublanes:
            p3[k * hrows:(k + 1) * hrows, :] = last

    def f_gate():
        gs = slice(2 * D, 3 * D)
        gate = jnp.dot(hbuf[...], a_w_in[:, gs], preferred_element_type=f32) + a_b_in[:, gs]
        sgate[...] = _silu_bf16(gate.astype(bf16))

    def f_ln(c):
        t0 = c * CHUNK
        cv = jnp.concatenate(
            [c3[pl.ds(k * TM + c, CHUNK, stride=SUB), :] for k in range(NBLK)], axis=1)
        mu = jnp.mean(cv, axis=-1, keepdims=True)
        xc = cv - mu
        ln = xc * lax.rsqrt(jnp.mean(xc * xc, axis=-1, keepdims=True) + EPS)
        ln = (ln * a_ln_g[...] + a_ln_b[...]).astype(bf16)
        cbuf[t0:t0 + CHUNK, :] = _silu_bf16(ln) * sgate[t0:t0 + CHUNK, :]

    def b_out_a():
        y = jnp.dot(csrc[...], a_w_out[...], preferred_element_type=f32) + a_b_out[...]
        x1 = xsrc[...] + y * _rstd(y) * a_post_g[...]
        x1buf[...] = x1
        xn = x1 * _rstd(x1)
        hkv[...] = (xn * kv_g[...]).astype(bf16)
        hb[...] = (xn * b_pre_g[...]).astype(bf16)

    def b_kv():
        kv = jnp.dot(hkv[...], w_kv[...], preferred_element_type=f32)
        kv_ref[...] = kv
        k0, k1 = _dup_heads(kv[:, 0:LANES])
        v0, v1 = _dup_heads(kv[:, LANES:2 * LANES])
        for hh, (kk, vv) in enumerate(((k0, v0), (k1, v1))):
            kd[hh, :, WINDOW:WINDOW + seg_rows, :] = kk.astype(bf16).reshape(nseg, seg_rows, LANES)
            vd[hh, :, WINDOW:WINDOW + seg_rows, :] = vv.astype(bf16).reshape(nseg, seg_rows, LANES)

    def b_q():
        q = jnp.dot(hb[...], b_w_in[:, 0:D], preferred_element_type=f32) * (HEAD_DIM ** -0.5 * LOG2E)
        lo_mask = lax.broadcasted_iota(jnp.int32, (TM, LANES), 1) < HEAD_DIM
        for blk in range(NBLK):
            qb = q[:, blk * LANES:(blk + 1) * LANES]
            kh, b = divmod(blk, GROUP // 2)
            lo = jnp.where(lo_mask, qb, 0.0).astype(bf16).reshape(NC, CHUNK, LANES)
            hi = jnp.where(lo_mask, 0.0, qb).astype(bf16).reshape(NC, CHUNK, LANES)
            qs[kh, :, b * LANES:b * LANES + CHUNK, :] = lo
            qs[kh, :, b * LANES + CHUNK:(b + 1) * LANES, :] = hi

    def b_gate():
        g2 = jnp.dot(hb[...], b_w_in[:, D:2 * D], preferred_element_type=f32)
        g2buf[...] = _silu_bf16(g2.astype(bf16))

    def b_attn(c):
        zkeys = jnp.zeros((KEYS - WINDOW - CHUNK, LANES), bf16)
        lo64 = lax.broadcasted_iota(jnp.int32, (CHUNK, LANES), 1) < HEAD_DIM
        seg, cl = divmod(c, cps)
        row0 = cl * CHUNK
        t0 = c * CHUNK
        if hist_ref is None:
            var = jnp.clip(tile * NC + c + first_bias, 0, 2)
        else:
            var = min(cl + first_bias, 2)
        pin = conv_done.get(PINS.get(c)) if skew else None
        outs = []
        for kh in range(N_KV):
            keys = kd[kh, seg, row0:row0 + WINDOW + CHUNK, :]
            if pin is not None:
                z16 = jnp.concatenate([pin, pin], axis=0).astype(bf16)
                keys = keys + jnp.concatenate([z16] * ((WINDOW + CHUNK) // (2 * SUB)), axis=0)
            kx = jnp.concatenate([keys, zkeys], axis=0)
            vx = jnp.concatenate(
                [vd[kh, seg, row0:row0 + WINDOW + CHUNK, :], zkeys], axis=0)
            s = lax.dot_general(qs[kh, c], kx, (((1,), (1,)), ((), ())),
                                preferred_element_type=f32)
            s = s + bias_ref[var, kh]
            m = jnp.max(s, axis=-1, keepdims=True)
            p = jnp.exp2(s - m)
            l = jnp.sum(p, axis=-1, keepdims=True)
            o = jnp.dot(p.astype(bf16), vx, preferred_element_type=f32)
            o = o * (1.0 / l)
            for b in range(GROUP // 2):
                outs.append(jnp.where(lo64,
                                      o[b * LANES:b * LANES + CHUNK, :],
                                      o[b * LANES + CHUNK:(b + 1) * LANES, :]))
        oraw[t0:t0 + CHUNK, :] = jnp.concatenate(outs, axis=1).astype(bf16)

    def b_out_b():
        y2 = jnp.dot(oraw[...] * g2buf[...], b_w_out[...], preferred_element_type=f32)
        y_ref[...] = x1buf[...] + y2 * _rstd(y2) * b_post_g[...]

    pieces = {"F0": f_norm, "Fg": f_gate, "W": b_out_a, "KV": b_kv, "Q": b_q,
              "G": b_gate, "O": b_out_b}
    for k in range(NBLK):
        pieces[f"F{k + 1}"] = functools.partial(f_proj, k)
        pieces[f"C{k}"] = functools.partial(f_conv, k)
    for c in range(NC):
        pieces[f"L{c}"] = functools.partial(f_ln, c)
        pieces[f"A{c}"] = functools.partial(b_attn, c)

    assert sorted(WOVEN) == sorted(PLAIN) == sorted(pieces)
    if not skew:
        for name in PLAIN:
            pieces[name]()
        return

    n_steps = pl.num_programs(0)
    is_front = lambda name: name[0] in "FCL"

    @pl.when(step == 0)
    def _():
        for name in PLAIN:
            if is_front(name):
                pieces[name]()

    @pl.when(jnp.logical_and(step > 0, step < n_steps - 1))
    def _():
        conv_done.clear()
        for name in WOVEN:
            pieces[name]()

    @pl.when(step == n_steps - 1)
    def _():
        conv_done.clear()
        for name in PLAIN:
            if not is_front(name):
                pieces[name]()

    @pl.when(step < n_steps - 1)
    def _():
        xsrc[...] = x_ref[...]


def _attn_bias(sinks):
    h = jnp.arange(1, N_HEADS + 1, dtype=f32)
    slopes = (2.0 ** (-8.0 * h / N_HEADS)).reshape(N_KV, GROUP, 1, 1)
    i = jnp.arange(CHUNK, dtype=jnp.int32)[:, None]
    j = jnp.arange(KEYS, dtype=jnp.int32)[None, :]
    dist = jnp.abs(WINDOW + i - j).astype(f32)[None, None]
    nkeys = WINDOW + CHUNK
    jj = j[None, None]
    base = jnp.where(jj < nkeys, -(slopes * dist) * LOG2E,
                     jnp.where(jj == nkeys,
                               sinks.astype(f32).reshape(N_KV, GROUP, 1, 1) * LOG2E, NEG_INF))
    variants = [jnp.where(jj < WINDOW - v * CHUNK, NEG_INF, base) for v in range(3)]
    return jnp.stack(variants, axis=0).reshape(3, N_KV, QROWS, KEYS)


def _const_spec(shape):
    nd = len(shape)
    return pl.BlockSpec(shape, lambda i, _nd=nd: (0,) * _nd,
                        pipeline_mode=pl.Buffered(1))


def _trunk(x2d, hist, kpre, vpre, weights, bias, seg_rows, first_bias):
    T = x2d.shape[0]
    nseg = TM // seg_rows
    n_tiles = T // TM
    has_state = hist is not None
    skew = not has_state and n_tiles > 1
    n_steps = n_tiles + 1 if skew else n_tiles
    hrows = NBLK * HIST * SUB

    if skew:
        front_tile = lambda i: jnp.minimum(i, n_tiles - 1)
        back_tile = lambda i: jnp.maximum(i - 1, 0)
    else:
        front_tile = back_tile = lambda i: i

    in_specs = [pl.BlockSpec((TM, D), lambda i: (front_tile(i), 0))]
    args = [x2d]
    if has_state:
        in_specs += [pl.BlockSpec((hrows, LANES), lambda i: (i, 0)),
                     pl.BlockSpec((nseg, WINDOW, LANES), lambda i: (i, 0, 0)),
                     pl.BlockSpec((nseg, WINDOW, LANES), lambda i: (i, 0, 0))]
        args += [hist, kpre, vpre]
    w_list = list(weights[:-2]) + [bias] + list(weights[-2:])
    in_specs += [_const_spec(w.shape) for w in w_list]
    args += w_list

    out_shape = (jax.ShapeDtypeStruct((T, D), f32),
                 jax.ShapeDtypeStruct((n_tiles * hrows, LANES), f32),
                 jax.ShapeDtypeStruct((T, 2 * LANES), f32))
    out_specs = (pl.BlockSpec((TM, D), lambda i: (back_tile(i), 0)),
                 pl.BlockSpec((hrows, LANES), lambda i: (front_tile(i), 0)),
                 pl.BlockSpec((TM, 2 * LANES), lambda i: (back_tile(i), 0)))
    scratch = [
        pltpu.VMEM((NBLK * TM, LANES), f32),
        pltpu.VMEM((hrows, LANES), f32),
        pltpu.VMEM((NBLK * TM, LANES), f32),
        pltpu.VMEM((TM, D), bf16),
        pltpu.VMEM((TM, D), bf16),
        pltpu.VMEM((TM, D), bf16),
        pltpu.VMEM((TM, D), bf16),
        pltpu.VMEM((TM, D), bf16),
        pltpu.VMEM((TM, D), f32),
        pltpu.VMEM((N_KV, nseg, seg_rows + WINDOW, LANES), bf16),
        pltpu.VMEM((N_KV, nseg, seg_rows + WINDOW, LANES), bf16),
        pltpu.VMEM((N_KV, NC, QROWS, LANES), bf16),
        pltpu.VMEM((TM, D), bf16),
        pltpu.VMEM((TM, D), bf16),
    ]
    if skew:
        scratch += [pltpu.VMEM((TM, D), bf16),
                    pltpu.VMEM((TM, D), f32)]

    if has_state:
        body = functools.partial(_trunk_kernel, seg_rows, first_bias, skew)
    else:
        def body(x_ref, *rest):
            _trunk_kernel(seg_rows, first_bias, skew, x_ref, None, None, None, *rest)

    return pl.pallas_call(
        body,
        grid=(n_steps,),
        in_specs=in_specs,
        out_specs=out_specs,
        out_shape=out_shape,
        scratch_shapes=scratch,
        compiler_params=pltpu.CompilerParams(
            dimension_semantics=("arbitrary",),
            vmem_limit_bytes=VMEM_LIMIT),
        name="trunk_sample" if has_state else "trunk_prompt",
    )(*args)


def kernel(x_prompt, x_sample, state_conv, cache_k, cache_v, a_pre_g, a_w_in, a_b_in, a_w_dw, a_b_dw, a_ln_g, a_ln_b, a_w_out, a_b_out, a_post_g, kv_g, w_kv, b_pre_g, b_w_in, b_sinks, b_w_out, b_post_g):
    row = lambda v: v.reshape(1, -1).astype(f32)
    w_dw = jnp.broadcast_to(a_w_dw[0].astype(f32).reshape(CONV_WIDTH, NBLK, 1, LANES),
                            (CONV_WIDTH, NBLK, SUB, LANES)).reshape(CONV_WIDTH * NBLK * SUB, LANES)
    b_dw = jnp.broadcast_to(a_b_dw[0].astype(f32).reshape(NBLK, 1, LANES),
                            (NBLK, SUB, LANES)).reshape(NBLK * SUB, LANES)
    weights = (row(a_pre_g[0]), a_w_in[0].astype(bf16), row(a_b_in[0]),
               w_dw, b_dw, row(a_ln_g[0]), row(a_ln_b[0]),
               a_w_out[0].astype(bf16), row(a_b_out[0]), row(a_post_g[0]),
               row(kv_g), w_kv.astype(bf16), row(b_pre_g[0]), b_w_in[0].astype(bf16),
               b_w_out[0].astype(bf16), row(b_post_g[0]))
    bias = _attn_bias(b_sinks[0])

    B, T, _ = x_prompt.shape
    yp, tail_p, kv_p = _trunk(x_prompt.reshape(B * T, D), None, None, None,
                              weights, bias, seg_rows=TM, first_bias=0)
    y_prompt = yp.reshape(B, T, D)
    tp = tail_p[-NBLK * HIST * SUB:].reshape(NBLK, HIST, SUB, LANES)[:, :, SUB - 1, :]
    conv_p = tp.transpose(1, 0, 2).reshape(1, 1, HIST, D)
    k_p = kv_p[-WINDOW:, 0:LANES].reshape(B, WINDOW, N_KV, HEAD_DIM)
    v_p = kv_p[-WINDOW:, LANES:].reshape(B, WINDOW, N_KV, HEAD_DIM)

    SB, ST, _ = x_sample.shape
    hist = state_conv[0].astype(f32).reshape(SB, HIST, NBLK, LANES).transpose(2, 1, 0, 3)
    hist = hist.reshape(NBLK * HIST * SUB, LANES)
    kpre = cache_k.astype(f32).reshape(SB, WINDOW, LANES)
    vpre = cache_v.astype(f32).reshape(SB, WINDOW, LANES)
    ys, tail_s, kv_s = _trunk(x_sample.reshape(SB * ST, D), hist, kpre, vpre,
                              weights, bias, seg_rows=ST, first_bias=2)
    y_sample = ys.reshape(SB, ST, D)
    conv_s = tail_s.reshape(NBLK, HIST, SB, LANES).transpose(2, 1, 0, 3).reshape(1, SB, HIST, D)
    k_new = kv_s[:, 0:LANES].reshape(SB, ST, N_KV, HEAD_DIM)
    v_new = kv_s[:, LANES:].reshape(SB, ST, N_KV, HEAD_DIM)
    k_s = jnp.concatenate([cache_k.astype(f32)[:, ST:], k_new], axis=1)
    v_s = jnp.concatenate([cache_v.astype(f32)[:, ST:], v_new], axis=1)
    return (y_prompt, y_sample, conv_p, conv_s, k_p, v_p, k_s, v_s)
```

```python
import functools
import math

import jax
import jax.numpy as jnp
from jax import lax
from jax.experimental import pallas as pl
from jax.experimental.pallas import tpu as pltpu

D = 1024
CHUNK = 64
CONV_WIDTH = 31
HIST = CONV_WIDTH - 1
N_HEADS = 16
N_KV = 2
GROUP = N_HEADS // N_KV
HEAD_DIM = 64
WINDOW = 128
EPS = 1e-6
NEG_INF = -1e30
LOG2E = math.log2(math.e)

LANES = 128
SUB = 8
NBLK = D // LANES
TM = SUB * CHUNK
NC = TM // CHUNK
KEYS = 256
QROWS = GROUP * CHUNK
TB = 16
VMEM_LIMIT = 58 * 1024 * 1024

WOVEN = ("F0 F1 F2 C0 F3 F4 C1 F5 F6 C2 F7 F8 C3 W C4 KV Q C5 G "
         "A0 A1 C6 A2 A3 Fg C7 A4 A5 A6 A7 O L0 L1 L2 L3 L4 L5 L6 L7").split()
PINS = {0: 5, 2: 6, 4: 7}
PLAIN = (["F0"] + [p for k in range(NBLK) for p in (f"F{k + 1}", f"C{k}")]
         + ["Fg"] + [f"L{c}" for c in range(NC)] + ["W", "KV", "Q"]
         + [f"A{c}" for c in range(NC // 2)] + ["G"]
         + [f"A{c}" for c in range(NC // 2, NC)] + ["O"])

f32 = jnp.float32
bf16 = jnp.bfloat16


def _sigmoid(x):
    return 1.0 / (1.0 + jnp.exp(-x))


def _silu_bf16(x):
    hx = x * 0.5
    return hx * jnp.tanh(hx) + hx


def _rstd(x):
    return lax.rsqrt(jnp.mean(x * x, axis=-1, keepdims=True) + EPS)


def _dup_heads(t):
    lane = lax.broadcasted_iota(jnp.int32, t.shape, 1)
    r = pltpu.roll(t, HEAD_DIM, axis=1)
    lo = lane < HEAD_DIM
    return jnp.where(lo, t, r), jnp.where(lo, r, t)


def _trunk_kernel(seg_rows, first_bias, skew,
                  x_ref, hist_ref, kpre_ref, vpre_ref,
                  a_w_in, a_b_in, a_w_dw, a_b_dw, a_ln_g, a_ln_b,
                  a_w_out, a_b_out, a_post_g, w_kv, b_w_in,
                  bias_ref, b_w_out, b_post_g,
                  y_ref, tail_ref, kv_ref,
                  u3, p3, c3, sgate, cbuf, hbuf, hb, x1buf, kd, vd, qs, g2buf, oraw,
                  *lagged):
    nseg = TM // seg_rows
    cps = seg_rows // CHUNK
    step = pl.program_id(0)
    tile = step - 1 if skew else step
    streams_on_sublanes = hist_ref is not None
    hrows = HIST * SUB
    if skew:
        csrc, xsrc = lagged
    else:
        csrc, xsrc = cbuf, x_ref
    conv_done = {}

    if hist_ref is None:
        @pl.when(step == 0)
        def _():
            p3[...] = jnp.zeros(p3.shape, f32)
            kd[...] = jnp.zeros(kd.shape, bf16)
            vd[...] = jnp.zeros(vd.shape, bf16)
            if skew:
                cbuf[...] = jnp.zeros(cbuf.shape, bf16)
                xsrc[...] = jnp.zeros(xsrc.shape, f32)

        if skew:
            csrc[...] = cbuf[...]
        for h in range(N_KV):
            kd[h, :, 0:WINDOW, :] = kd[h, :, seg_rows:seg_rows + WINDOW, :]
            vd[h, :, 0:WINDOW, :] = vd[h, :, seg_rows:seg_rows + WINDOW, :]
    else:
        k0, k1 = _dup_heads(kpre_ref[...].reshape(nseg * WINDOW, LANES))
        v0, v1 = _dup_heads(vpre_ref[...].reshape(nseg * WINDOW, LANES))
        kd[0, :, 0:WINDOW, :] = k0.astype(bf16).reshape(nseg, WINDOW, LANES)
        kd[1, :, 0:WINDOW, :] = k1.astype(bf16).reshape(nseg, WINDOW, LANES)
        vd[0, :, 0:WINDOW, :] = v0.astype(bf16).reshape(nseg, WINDOW, LANES)
        vd[1, :, 0:WINDOW, :] = v1.astype(bf16).reshape(nseg, WINDOW, LANES)

    def f_norm():
        x = x_ref[...]
        hbuf[...] = (x * _rstd(x)).astype(bf16)

    def f_proj(k):
        ca = slice(LANES * k, LANES * (k + 1))
        cg = slice(D + LANES * k, D + LANES * (k + 1))
        wk = jnp.concatenate([a_w_in[:, ca], a_w_in[:, cg]], axis=1)
        bk = jnp.concatenate([a_b_in[:, ca], a_b_in[:, cg]], axis=1)
        zk = jnp.dot(hbuf[...], wk, preferred_element_type=f32) + bk
        uk = zk[:, 0:LANES] * _sigmoid(zk[:, LANES:2 * LANES])
        for s in range(SUB):
            u3[pl.ds(k * TM + s, CHUNK, stride=SUB), :] = uk[s * CHUNK:(s + 1) * CHUNK, :]

    def f_conv(k):
        sub0 = lax.broadcasted_iota(jnp.int32, (SUB, LANES), 0) == 0

        def tile_in(ap):
            if ap >= 0:
                return u3[k * TM + ap * SUB:k * TM + (ap + 1) * SUB, :]
            r0 = k * hrows + (ap + HIST) * SUB
            if streams_on_sublanes:
                return hist_ref[r0:r0 + SUB, :]
            a = ap + CHUNK
            cur = pltpu.roll(u3[k * TM + a * SUB:k * TM + (a + 1) * SUB, :], 1, axis=0)
            prv = pltpu.roll(p3[r0:r0 + SUB, :], 1, axis=0)
            return jnp.where(sub0, prv, cur)

        bias_k = a_b_dw[k * SUB:(k + 1) * SUB, :]
        for a0 in range(0, CHUNK, TB):
            rows = [tile_in(a0 - HIST + d) for d in range(TB + HIST)]
            accs = [bias_k] * TB
            for j in range(CONV_WIDTH):
                wj = a_w_dw[(j * NBLK + k) * SUB:(j * NBLK + k + 1) * SUB, :]
                for t in range(TB):
                    accs[t] = accs[t] + rows[t + j] * wj
            for t in range(TB):
                c3[k * TM + (a0 + t) * SUB:k * TM + (a0 + t + 1) * SUB, :] = accs[t]
        conv_done[k] = jnp.minimum(jnp.abs(accs[TB - 1]), 0.0)

        last = u3[k * TM + (CHUNK - HIST) * SUB:(k + 1) * TM, :]
        tail_ref[k * hrows:(k + 1) * hrows, :] = last
        if not streams_on_sublanes:
            p3[k * hrows:(k + 1) * hrows, :] = last

    def f_gate():
        gs = slice(2 * D, 3 * D)
        gate = jnp.dot(hbuf[...], a_w_in[:, gs], preferred_element_type=f32) + a_b_in[:, gs]
        sgate[...] = _silu_bf16(gate.astype(bf16))

    def f_ln(c):
        t0 = c * CHUNK
        cv = jnp.concatenate(
            [c3[pl.ds(k * TM + c, CHUNK, stride=SUB), :] for k in range(NBLK)], axis=1)
        mu = jnp.mean(cv, axis=-1, keepdims=True)
        xc = cv - mu
        ln = xc * lax.rsqrt(jnp.mean(xc * xc, axis=-1, keepdims=True) + EPS)
        ln = (ln * a_ln_g[...] + a_ln_b[...]).astype(bf16)
        cbuf[t0:t0 + CHUNK, :] = _silu_bf16(ln) * sgate[t0:t0 + CHUNK, :]

    def b_out_a():
        y = jnp.dot(csrc[...], a_w_out[...], preferred_element_type=f32) + a_b_out[...]
        x1 = xsrc[...] + y * _rstd(y) * a_post_g[...]
        x1buf[...] = x1
        hb[...] = (x1 * _rstd(x1)).astype(bf16)

    def b_kv():
        kv = jnp.dot(hb[...], w_kv[...], preferred_element_type=f32)
        kv_ref[...] = kv
        k0, k1 = _dup_heads(kv[:, 0:LANES])
        v0, v1 = _dup_heads(kv[:, LANES:2 * LANES])
        for hh, (kk, vv) in enumerate(((k0, v0), (k1, v1))):
            kd[hh, :, WINDOW:WINDOW + seg_rows, :] = kk.astype(bf16).reshape(nseg, seg_rows, LANES)
            vd[hh, :, WINDOW:WINDOW + seg_rows, :] = vv.astype(bf16).reshape(nseg, seg_rows, LANES)

    def b_q():
        q = jnp.dot(hb[...], b_w_in[:, 0:D], preferred_element_type=f32).astype(bf16)
        lo_mask = lax.broadcasted_iota(jnp.int32, (TM, LANES), 1) < HEAD_DIM
        keep_lo = jnp.where(lo_mask, 1.0, 0.0).astype(bf16)
        keep_hi = jnp.where(lo_mask, 0.0, 1.0).astype(bf16)
        for blk in range(NBLK):
            qb = q[:, blk * LANES:(blk + 1) * LANES]
            kh, b = divmod(blk, GROUP // 2)
            qs[kh, :, b * LANES:b * LANES + CHUNK, :] = (qb * keep_lo).reshape(NC, CHUNK, LANES)
            qs[kh, :, b * LANES + CHUNK:(b + 1) * LANES, :] = (qb * keep_hi).reshape(NC, CHUNK, LANES)

    def b_gate():
        g2 = jnp.dot(hb[...], b_w_in[:, D:2 * D], preferred_element_type=f32)
        g2buf[...] = _silu_bf16(g2.astype(bf16))

    def b_attn(c):
        zkeys = jnp.zeros((KEYS - WINDOW - CHUNK, LANES), bf16)
        lo64 = lax.broadcasted_iota(jnp.int32, (CHUNK, LANES), 1) < HEAD_DIM
        seg, cl = divmod(c, cps)
        row0 = cl * CHUNK
        t0 = c * CHUNK
        if hist_ref is None:
            var = jnp.clip(tile * NC + c + first_bias, 0, 2)
        else:
            var = min(cl + first_bias, 2)
        pin = conv_done.get(PINS.get(c)) if skew else None
        outs = []
        for kh in range(N_KV):
            keys = kd[kh, seg, row0:row0 + WINDOW + CHUNK, :]
            if pin is not None:
                z16 = jnp.concatenate([pin, pin], axis=0).astype(bf16)
                keys = keys + jnp.concatenate([z16] * ((WINDOW + CHUNK) // (2 * SUB)), axis=0)
            kx = jnp.concatenate([keys, zkeys], axis=0)
            vx = jnp.concatenate(
                [vd[kh, seg, row0:row0 + WINDOW + CHUNK, :], zkeys], axis=0)
            s = lax.dot_general(qs[kh, c], kx, (((1,), (1,)), ((), ())),
                                preferred_element_type=f32)
            s = s + bias_ref[var, kh]
            m = jnp.max(s, axis=-1, keepdims=True)
            p = jnp.exp2(s - m)
            l = jnp.sum(p, axis=-1, keepdims=True)
            o = jnp.dot(p.astype(bf16), vx, preferred_element_type=f32)
            o = o * (1.0 / l)
            for b in range(GROUP // 2):
                outs.append(jnp.where(lo64,
                                      o[b * LANES:b * LANES + CHUNK, :],
                                      o[b * LANES + CHUNK:(b + 1) * LANES, :]))
        oraw[t0:t0 + CHUNK, :] = jnp.concatenate(outs, axis=1).astype(bf16)

    def b_out_b():
        y2 = jnp.dot(oraw[...] * g2buf[...], b_w_out[...], preferred_element_type=f32)
        y_ref[...] = x1buf[...] + y2 * _rstd(y2) * b_post_g[...]

    pieces = {"F0": f_norm, "Fg": f_gate, "W": b_out_a, "KV": b_kv, "Q": b_q,
              "G": b_gate, "O": b_out_b}
    for k in range(NBLK):
        pieces[f"F{k + 1}"] = functools.partial(f_proj, k)
        pieces[f"C{k}"] = functools.partial(f_conv, k)
    for c in range(NC):
        pieces[f"L{c}"] = functools.partial(f_ln, c)
        pieces[f"A{c}"] = functools.partial(b_attn, c)

    order = WOVEN if skew else PLAIN
    assert sorted(order) == sorted(pieces)
    for name in order:
        pieces[name]()
    if skew:
        xsrc[...] = x_ref[...]


def _attn_bias(sinks):
    h = jnp.arange(1, N_HEADS + 1, dtype=f32)
    slopes = (2.0 ** (-8.0 * h / N_HEADS)).reshape(N_KV, GROUP, 1, 1)
    i = jnp.arange(CHUNK, dtype=jnp.int32)[:, None]
    j = jnp.arange(KEYS, dtype=jnp.int32)[None, :]
    dist = jnp.abs(WINDOW + i - j).astype(f32)[None, None]
    nkeys = WINDOW + CHUNK
    jj = j[None, None]
    base = jnp.where(jj < nkeys, -(slopes * dist) * LOG2E,
                     jnp.where(jj == nkeys,
                               sinks.astype(f32).reshape(N_KV, GROUP, 1, 1) * LOG2E, NEG_INF))
    variants = [jnp.where(jj < WINDOW - v * CHUNK, NEG_INF, base) for v in range(3)]
    return jnp.stack(variants, axis=0).reshape(3, N_KV, QROWS, KEYS)


def _const_spec(shape):
    nd = len(shape)
    return pl.BlockSpec(shape, lambda i, _nd=nd: (0,) * _nd,
                        pipeline_mode=pl.Buffered(1))


def _trunk(x2d, hist, kpre, vpre, weights, bias, seg_rows, first_bias):
    T = x2d.shape[0]
    nseg = TM // seg_rows
    n_tiles = T // TM
    has_state = hist is not None
    skew = not has_state and n_tiles > 1
    n_steps = n_tiles + 1 if skew else n_tiles
    hrows = NBLK * HIST * SUB

    if skew:
        front_tile = lambda i: jnp.minimum(i, n_tiles - 1)
        back_tile = lambda i: jnp.maximum(i - 1, 0)
    else:
        front_tile = back_tile = lambda i: i

    in_specs = [pl.BlockSpec((TM, D), lambda i: (front_tile(i), 0))]
    args = [x2d]
    if has_state:
        in_specs += [pl.BlockSpec((hrows, LANES), lambda i: (i, 0)),
                     pl.BlockSpec((nseg, WINDOW, LANES), lambda i: (i, 0, 0)),
                     pl.BlockSpec((nseg, WINDOW, LANES), lambda i: (i, 0, 0))]
        args += [hist, kpre, vpre]
    w_list = list(weights[:-2]) + [bias] + list(weights[-2:])
    in_specs += [_const_spec(w.shape) for w in w_list]
    args += w_list

    out_shape = (jax.ShapeDtypeStruct((T, D), f32),
                 jax.ShapeDtypeStruct((n_tiles * hrows, LANES), f32),
                 jax.ShapeDtypeStruct((T, 2 * LANES), f32))
    out_specs = (pl.BlockSpec((TM, D), lambda i: (back_tile(i), 0)),
                 pl.BlockSpec((hrows, LANES), lambda i: (front_tile(i), 0)),
                 pl.BlockSpec((TM, 2 * LANES), lambda i: (back_tile(i), 0)))
    scratch = [
        pltpu.VMEM((NBLK * TM, LANES), f32),
        pltpu.VMEM((hrows, LANES), f32),
        pltpu.VMEM((NBLK * TM, LANES), f32),
        pltpu.VMEM((TM, D), bf16),
        pltpu.VMEM((TM, D), bf16),
        pltpu.VMEM((TM, D), bf16),
        pltpu.VMEM((TM, D), bf16),
        pltpu.VMEM((TM, D), f32),
        pltpu.VMEM((N_KV, nseg, seg_rows + WINDOW, LANES), bf16),
        pltpu.VMEM((N_KV, nseg, seg_rows + WINDOW, LANES), bf16),
        pltpu.VMEM((N_KV, NC, QROWS, LANES), bf16),
        pltpu.VMEM((TM, D), bf16),
        pltpu.VMEM((TM, D), bf16),
    ]
    if skew:
        scratch += [pltpu.VMEM((TM, D), bf16),
                    pltpu.VMEM((TM, D), f32)]

    if has_state:
        body = functools.partial(_trunk_kernel, seg_rows, first_bias, skew)
    else:
        def body(x_ref, *rest):
            _trunk_kernel(seg_rows, first_bias, skew, x_ref, None, None, None, *rest)

    return pl.pallas_call(
        body,
        grid=(n_steps,),
        in_specs=in_specs,
        out_specs=out_specs,
        out_shape=out_shape,
        scratch_shapes=scratch,
        compiler_params=pltpu.CompilerParams(
            dimension_semantics=("arbitrary",),
            vmem_limit_bytes=VMEM_LIMIT),
        name="trunk_sample" if has_state else "trunk_prompt",
    )(*args)


def kernel(x_prompt, x_sample, state_conv, cache_k, cache_v, a_pre_g, a_w_in, a_b_in, a_w_dw, a_b_dw, a_ln_g, a_ln_b, a_w_out, a_b_out, a_post_g, kv_g, w_kv, b_pre_g, b_w_in, b_sinks, b_w_out, b_post_g):
    row = lambda v: v.reshape(1, -1).astype(f32)
    w_dw = jnp.broadcast_to(a_w_dw[0].astype(f32).reshape(CONV_WIDTH, NBLK, 1, LANES),
                            (CONV_WIDTH, NBLK, SUB, LANES)).reshape(CONV_WIDTH * NBLK * SUB, LANES)
    b_dw = jnp.broadcast_to(a_b_dw[0].astype(f32).reshape(NBLK, 1, LANES),
                            (NBLK, SUB, LANES)).reshape(NBLK * SUB, LANES)
    col = lambda v: v.astype(f32).reshape(-1, 1)
    qscale = jnp.concatenate([jnp.full((1, D), HEAD_DIM ** -0.5 * LOG2E, f32),
                              jnp.ones((1, D), f32)], axis=1)
    weights = ((col(a_pre_g[0]) * a_w_in[0]).astype(bf16), row(a_b_in[0]),
               w_dw, b_dw, row(a_ln_g[0]), row(a_ln_b[0]),
               a_w_out[0].astype(bf16), row(a_b_out[0]), row(a_post_g[0]),
               (col(kv_g) * w_kv).astype(bf16),
               (col(b_pre_g[0]) * b_w_in[0] * qscale).astype(bf16),
               b_w_out[0].astype(bf16), row(b_post_g[0]))
    bias = _attn_bias(b_sinks[0])

    B, T, _ = x_prompt.shape
    yp, tail_p, kv_p = _trunk(x_prompt.reshape(B * T, D), None, None, None,
                              weights, bias, seg_rows=TM, first_bias=0)
    y_prompt = yp.reshape(B, T, D)
    tp = tail_p[-NBLK * HIST * SUB:].reshape(NBLK, HIST, SUB, LANES)[:, :, SUB - 1, :]
    conv_p = tp.transpose(1, 0, 2).reshape(1, 1, HIST, D)
    k_p = kv_p[-WINDOW:, 0:LANES].reshape(B, WINDOW, N_KV, HEAD_DIM)
    v_p = kv_p[-WINDOW:, LANES:].reshape(B, WINDOW, N_KV, HEAD_DIM)

    SB, ST, _ = x_sample.shape
    hist = state_conv[0].astype(f32).reshape(SB, HIST, NBLK, LANES).transpose(2, 1, 0, 3)
    hist = hist.reshape(NBLK * HIST * SUB, LANES)
    kpre = cache_k.astype(f32).reshape(SB, WINDOW, LANES)
    vpre = cache_v.astype(f32).reshape(SB, WINDOW, LANES)
    ys, tail_s, kv_s = _trunk(x_sample.reshape(SB * ST, D), hist, kpre, vpre,
                              weights, bias, seg_rows=ST, first_bias=2)
    y_sample = ys.reshape(SB, ST, D)
    conv_s = tail_s.reshape(NBLK, HIST, SB, LANES).transpose(2, 1, 0, 3).reshape(1, SB, HIST, D)
    k_new = kv_s[:, 0:LANES].reshape(SB, ST, N_KV, HEAD_DIM)
    v_new = kv_s[:, LANES:].reshape(SB, ST, N_KV, HEAD_DIM)
    k_s = jnp.concatenate([cache_k.astype(f32)[:, ST:], k_new], axis=1)
    v_s = jnp.concatenate([cache_v.astype(f32)[:, ST:], v_new], axis=1)
    return (y_prompt, y_sample, conv_p, conv_s, k_p, v_p, k_s, v_s)
```

```python
import functools
import math

import jax
import jax.numpy as jnp
from jax import lax
from jax.experimental import pallas as pl
from jax.experimental.pallas import tpu as pltpu

D = 1024
CHUNK = 64
CONV_WIDTH = 31
HIST = CONV_WIDTH - 1
N_HEADS = 16
N_KV = 2
GROUP = N_HEADS // N_KV
HEAD_DIM = 64
WINDOW = 128
EPS = 1e-6
NEG_INF = -1e30
LOG2E = math.log2(math.e)

LANES = 128
SUB = 8
NBLK = D // LANES
TM = SUB * CHUNK
NC = TM // CHUNK
KEYS = 256
QROWS = GROUP * CHUNK
TB = 16
VMEM_LIMIT = 58 * 1024 * 1024

WOVEN = ("F0 F1 F2 C0 F3 F4 C1 F5 F6 C2 F7 F8 C3 W C4 KV Q C5 G "
         "A0 A1 C6 A2 A3 Fg C7 A4 A5 A6 A7 O L0 L1 L2 L3 L4 L5 L6 L7").split()
PINS = {0: 5, 2: 6, 4: 7}
PLAIN = (["F0"] + [p for k in range(NBLK) for p in (f"F{k + 1}", f"C{k}")]
         + ["Fg"] + [f"L{c}" for c in range(NC)] + ["W", "KV", "Q"]
         + [f"A{c}" for c in range(NC // 2)] + ["G"]
         + [f"A{c}" for c in range(NC // 2, NC)] + ["O"])

f32 = jnp.float32
bf16 = jnp.bfloat16


def _sigmoid(x):
    return 1.0 / (1.0 + jnp.exp(-x))


def _silu_bf16(x):
    hx = x * 0.5
    return hx * jnp.tanh(hx) + hx


def _rstd(x):
    return lax.rsqrt(jnp.mean(x * x, axis=-1, keepdims=True) + EPS)


def _dup_heads(t):
    lane = lax.broadcasted_iota(jnp.int32, t.shape, 1)
    r = pltpu.roll(t, HEAD_DIM, axis=1)
    lo = lane < HEAD_DIM
    return jnp.where(lo, t, r), jnp.where(lo, r, t)


def _trunk_kernel(seg_rows, first_bias, skew,
                  x_ref, hist_ref, kpre_ref, vpre_ref,
                  a_w_in, a_b_in, a_w_dw, a_b_dw, a_ln_g, a_ln_b,
                  a_w_out, a_b_out, a_post_g, w_kv, b_w_in,
                  sinks_ref, b_w_out, b_post_g,
                  y_ref, tail_ref, kv_ref, ks_ref, vs_ref,
                  u3, p3, c3, sgate, cbuf, hbuf, hb, x1buf, kd, vd, qs, g2buf, oraw, bias_ref,
                  *lagged):
    nseg = TM // seg_rows
    cps = seg_rows // CHUNK
    step = pl.program_id(0)
    tile = step - 1 if skew else step
    streams_on_sublanes = hist_ref is not None
    hrows = HIST * SUB
    if skew:
        csrc, xsrc = lagged
    else:
        csrc, xsrc = cbuf, x_ref
    conv_done = {}

    @pl.when(step == 0)
    def _():
        i = lax.broadcasted_iota(jnp.int32, (CHUNK, KEYS), 0)
        j = lax.broadcasted_iota(jnp.int32, (CHUNK, KEYS), 1)
        dist = jnp.abs(WINDOW + i - j).astype(f32)
        nkeys = WINDOW + CHUNK
        for hd in range(N_HEADS):
            kh, g = divmod(hd, GROUP)
            slope = 2.0 ** (-8.0 * (hd + 1) / N_HEADS)
            t = jnp.where(j < nkeys, dist * (-slope * LOG2E),
                          jnp.where(j == nkeys, sinks_ref[hd] * LOG2E, NEG_INF))
            for v in range(first_bias, 3):
                bias_ref[v, kh, g * CHUNK:(g + 1) * CHUNK, :] = (
                    jnp.where(j < WINDOW - v * CHUNK, NEG_INF, t))

    if hist_ref is None:
        @pl.when(step == 0)
        def _():
            p3[...] = jnp.zeros(p3.shape, f32)
            kd[...] = jnp.zeros(kd.shape, bf16)
            vd[...] = jnp.zeros(vd.shape, bf16)
            if skew:
                cbuf[...] = jnp.zeros(cbuf.shape, bf16)
                xsrc[...] = jnp.zeros(xsrc.shape, f32)

        if skew:
            csrc[...] = cbuf[...]
        for h in range(N_KV):
            kd[h, :, 0:WINDOW, :] = kd[h, :, seg_rows:seg_rows + WINDOW, :]
            vd[h, :, 0:WINDOW, :] = vd[h, :, seg_rows:seg_rows + WINDOW, :]
    else:
        k0, k1 = _dup_heads(kpre_ref[...].reshape(nseg * WINDOW, LANES))
        v0, v1 = _dup_heads(vpre_ref[...].reshape(nseg * WINDOW, LANES))
        kd[0, :, 0:WINDOW, :] = k0.astype(bf16).reshape(nseg, WINDOW, LANES)
        kd[1, :, 0:WINDOW, :] = k1.astype(bf16).reshape(nseg, WINDOW, LANES)
        vd[0, :, 0:WINDOW, :] = v0.astype(bf16).reshape(nseg, WINDOW, LANES)
        vd[1, :, 0:WINDOW, :] = v1.astype(bf16).reshape(nseg, WINDOW, LANES)

    def f_norm():
        x = x_ref[...]
        hbuf[...] = (x * _rstd(x)).astype(bf16)

    def f_proj(k):
        ca = slice(LANES * k, LANES * (k + 1))
        cg = slice(D + LANES * k, D + LANES * (k + 1))
        wk = jnp.concatenate([a_w_in[:, ca], a_w_in[:, cg]], axis=1)
        bk = jnp.concatenate([a_b_in[:, ca], a_b_in[:, cg]], axis=1)
        zk = jnp.dot(hbuf[...], wk, preferred_element_type=f32) + bk
        uk = zk[:, 0:LANES] * _sigmoid(zk[:, LANES:2 * LANES])
        for s in range(SUB):
            u3[pl.ds(k * TM + s, CHUNK, stride=SUB), :] = uk[s * CHUNK:(s + 1) * CHUNK, :]

    def f_conv(k):
        sub0 = lax.broadcasted_iota(jnp.int32, (SUB, LANES), 0) == 0

        def tile_in(ap):
            if ap >= 0:
                return u3[k * TM + ap * SUB:k * TM + (ap + 1) * SUB, :]
            r0 = k * hrows + (ap + HIST) * SUB
            if streams_on_sublanes:
                return hist_ref[r0:r0 + SUB, :]
            a = ap + CHUNK
            cur = pltpu.roll(u3[k * TM + a * SUB:k * TM + (a + 1) * SUB, :], 1, axis=0)
            prv = pltpu.roll(p3[r0:r0 + SUB, :], 1, axis=0)
            return jnp.where(sub0, prv, cur)

        bias_k = a_b_dw[k * SUB:(k + 1) * SUB, :]
        for a0 in range(0, CHUNK, TB):
            rows = [tile_in(a0 - HIST + d) for d in range(TB + HIST)]
            accs = [bias_k] * TB
            for j in range(CONV_WIDTH):
                wj = a_w_dw[(j * NBLK + k) * SUB:(j * NBLK + k + 1) * SUB, :]
                for t in range(TB):
                    accs[t] = accs[t] + rows[t + j] * wj
            for t in range(TB):
                c3[k * TM + (a0 + t) * SUB:k * TM + (a0 + t + 1) * SUB, :] = accs[t]
        conv_done[k] = jnp.minimum(jnp.abs(accs[TB - 1]), 0.0)

        last = u3[k * TM + (CHUNK - HIST) * SUB:(k + 1) * TM, :]
        tail_ref[k * hrows:(k + 1) * hrows, :] = last
        if not streams_on_sublanes:
            p3[k * hrows:(k + 1) * hrows, :] = last

    def f_gate():
        gs = slice(2 * D, 3 * D)
        gate = jnp.dot(hbuf[...], a_w_in[:, gs], preferred_element_type=f32) + a_b_in[:, gs]
        sgate[...] = _silu_bf16(gate.astype(bf16))

    def f_ln(c):
        t0 = c * CHUNK
        cv = jnp.concatenate(
            [c3[pl.ds(k * TM + c, CHUNK, stride=SUB), :] for k in range(NBLK)], axis=1)
        mu = jnp.mean(cv, axis=-1, keepdims=True)
        xc = cv - mu
        ln = xc * lax.rsqrt(jnp.mean(xc * xc, axis=-1, keepdims=True) + EPS)
        ln = (ln * a_ln_g[...] + a_ln_b[...]).astype(bf16)
        cbuf[t0:t0 + CHUNK, :] = _silu_bf16(ln) * sgate[t0:t0 + CHUNK, :]

    def b_out_a():
        y = jnp.dot(csrc[...], a_w_out[...], preferred_element_type=f32) + a_b_out[...]
        x1 = xsrc[...] + y * _rstd(y) * a_post_g[...]
        x1buf[...] = x1
        hb[...] = (x1 * _rstd(x1)).astype(bf16)

    def b_kv():
        kv = jnp.dot(hb[...], w_kv[...], preferred_element_type=f32)
        kv_ref[...] = kv
        if ks_ref is not None:
            for dst, pre, new in ((ks_ref, kpre_ref, kv[:, 0:LANES]),
                                  (vs_ref, vpre_ref, kv[:, LANES:2 * LANES])):
                dst[:, 0:WINDOW - seg_rows, :] = pre[:, seg_rows:, :]
                dst[:, WINDOW - seg_rows:, :] = new.reshape(nseg, seg_rows, LANES)
        k0, k1 = _dup_heads(kv[:, 0:LANES])
        v0, v1 = _dup_heads(kv[:, LANES:2 * LANES])
        for hh, (kk, vv) in enumerate(((k0, v0), (k1, v1))):
            kd[hh, :, WINDOW:WINDOW + seg_rows, :] = kk.astype(bf16).reshape(nseg, seg_rows, LANES)
            vd[hh, :, WINDOW:WINDOW + seg_rows, :] = vv.astype(bf16).reshape(nseg, seg_rows, LANES)

    def b_q():
        q = jnp.dot(hb[...], b_w_in[:, 0:D], preferred_element_type=f32).astype(bf16)
        lo_mask = lax.broadcasted_iota(jnp.int32, (TM, LANES), 1) < HEAD_DIM
        keep_lo = jnp.where(lo_mask, 1.0, 0.0).astype(bf16)
        keep_hi = jnp.where(lo_mask, 0.0, 1.0).astype(bf16)
        for blk in range(NBLK):
            qb = q[:, blk * LANES:(blk + 1) * LANES]
            kh, b = divmod(blk, GROUP // 2)
            qs[kh, :, b * LANES:b * LANES + CHUNK, :] = (qb * keep_lo).reshape(NC, CHUNK, LANES)
            qs[kh, :, b * LANES + CHUNK:(b + 1) * LANES, :] = (qb * keep_hi).reshape(NC, CHUNK, LANES)

    def b_gate():
        g2 = jnp.dot(hb[...], b_w_in[:, D:2 * D], preferred_element_type=f32)
        g2buf[...] = _silu_bf16(g2.astype(bf16))

    def b_attn(c):
        zkeys = jnp.zeros((KEYS - WINDOW - CHUNK, LANES), bf16)
        lo64 = lax.broadcasted_iota(jnp.int32, (CHUNK, LANES), 1) < HEAD_DIM
        seg, cl = divmod(c, cps)
        row0 = cl * CHUNK
        t0 = c * CHUNK
        if hist_ref is None:
            var = jnp.clip(tile * NC + c + first_bias, 0, 2)
        else:
            var = min(cl + first_bias, 2)
        pin = conv_done.get(PINS.get(c)) if skew else None
        outs = []
        for kh in range(N_KV):
            keys = kd[kh, seg, row0:row0 + WINDOW + CHUNK, :]
            if pin is not None:
                z16 = jnp.concatenate([pin, pin], axis=0).astype(bf16)
                keys = keys + jnp.concatenate([z16] * ((WINDOW + CHUNK) // (2 * SUB)), axis=0)
            kx = jnp.concatenate([keys, zkeys], axis=0)
            vx = jnp.concatenate(
                [vd[kh, seg, row0:row0 + WINDOW + CHUNK, :], zkeys], axis=0)
            s = lax.dot_general(qs[kh, c], kx, (((1,), (1,)), ((), ())),
                                preferred_element_type=f32)
            s = s + bias_ref[var, kh]
            m = jnp.max(s, axis=-1, keepdims=True)
            p = jnp.exp2(s - m)
            l = jnp.sum(p, axis=-1, keepdims=True)
            o = jnp.dot(p.astype(bf16), vx, preferred_element_type=f32)
            o = o * (1.0 / l)
            for b in range(GROUP // 2):
                outs.append(jnp.where(lo64,
                                      o[b * LANES:b * LANES + CHUNK, :],
                                      o[b * LANES + CHUNK:(b + 1) * LANES, :]))
        oraw[t0:t0 + CHUNK, :] = jnp.concatenate(outs, axis=1).astype(bf16)

    def b_out_b():
        y2 = jnp.dot(oraw[...] * g2buf[...], b_w_out[...], preferred_element_type=f32)
        y_ref[...] = x1buf[...] + y2 * _rstd(y2) * b_post_g[...]

    pieces = {"F0": f_norm, "Fg": f_gate, "W": b_out_a, "KV": b_kv, "Q": b_q,
              "G": b_gate, "O": b_out_b}
    for k in range(NBLK):
        pieces[f"F{k + 1}"] = functools.partial(f_proj, k)
        pieces[f"C{k}"] = functools.partial(f_conv, k)
    for c in range(NC):
        pieces[f"L{c}"] = functools.partial(f_ln, c)
        pieces[f"A{c}"] = functools.partial(b_attn, c)

    order = WOVEN if skew else PLAIN
    assert sorted(order) == sorted(pieces)
    for name in order:
        pieces[name]()
    if skew:
        xsrc[...] = x_ref[...]


def _const_spec(shape):
    nd = len(shape)
    return pl.BlockSpec(shape, lambda i, _nd=nd: (0,) * _nd,
                        pipeline_mode=pl.Buffered(1))


def _trunk(x2d, hist, kpre, vpre, weights, sinks, seg_rows, first_bias):
    T = x2d.shape[0]
    nseg = TM // seg_rows
    n_tiles = T // TM
    has_state = hist is not None
    skew = not has_state and n_tiles > 1
    n_steps = n_tiles + 1 if skew else n_tiles
    hrows = NBLK * HIST * SUB

    if skew:
        front_tile = lambda i: jnp.minimum(i, n_tiles - 1)
        back_tile = lambda i: jnp.maximum(i - 1, 0)
    else:
        front_tile = back_tile = lambda i: i

    in_specs = [pl.BlockSpec((TM, D), lambda i: (front_tile(i), 0))]
    args = [x2d]
    if has_state:
        in_specs += [pl.BlockSpec((hrows, LANES), lambda i: (i, 0)),
                     pl.BlockSpec((nseg, WINDOW, LANES), lambda i: (i, 0, 0)),
                     pl.BlockSpec((nseg, WINDOW, LANES), lambda i: (i, 0, 0))]
        args += [hist, kpre, vpre]
    in_specs += ([_const_spec(w.shape) for w in weights[:-2]]
                 + [pl.BlockSpec(memory_space=pltpu.SMEM)]
                 + [_const_spec(w.shape) for w in weights[-2:]])
    args += list(weights[:-2]) + [sinks] + list(weights[-2:])
    n_in_out = len(weights) + 1 + 3

    out_shape = [jax.ShapeDtypeStruct((T, D), f32),
                 jax.ShapeDtypeStruct((n_tiles * hrows, LANES), f32),
                 jax.ShapeDtypeStruct((T, 2 * LANES), f32)]
    out_specs = [pl.BlockSpec((TM, D), lambda i: (back_tile(i), 0)),
                 pl.BlockSpec((hrows, LANES), lambda i: (front_tile(i), 0)),
                 pl.BlockSpec((TM, 2 * LANES), lambda i: (back_tile(i), 0))]
    if has_state:
        out_shape += [jax.ShapeDtypeStruct((n_tiles * nseg, WINDOW, LANES), f32)] * 2
        out_specs += [pl.BlockSpec((nseg, WINDOW, LANES), lambda i: (i, 0, 0))] * 2
    scratch = [
        pltpu.VMEM((NBLK * TM, LANES), f32),
        pltpu.VMEM((hrows, LANES), f32),
        pltpu.VMEM((NBLK * TM, LANES), f32),
        pltpu.VMEM((TM, D), bf16),
        pltpu.VMEM((TM, D), bf16),
        pltpu.VMEM((TM, D), bf16),
        pltpu.VMEM((TM, D), bf16),
        pltpu.VMEM((TM, D), f32),
        pltpu.VMEM((N_KV, nseg, seg_rows + WINDOW, LANES), bf16),
        pltpu.VMEM((N_KV, nseg, seg_rows + WINDOW, LANES), bf16),
        pltpu.VMEM((N_KV, NC, QROWS, LANES), bf16),
        pltpu.VMEM((TM, D), bf16),
        pltpu.VMEM((TM, D), bf16),
        pltpu.VMEM((3, N_KV, QROWS, KEYS), f32),
    ]
    if skew:
        scratch += [pltpu.VMEM((TM, D), bf16),
                    pltpu.VMEM((TM, D), f32)]

    if has_state:
        body = functools.partial(_trunk_kernel, seg_rows, first_bias, skew)
    else:
        def body(x_ref, *rest):
            _trunk_kernel(seg_rows, first_bias, skew, x_ref, None, None, None,
                          *rest[:n_in_out], None, None, *rest[n_in_out:])

    return pl.pallas_call(
        body,
        grid=(n_steps,),
        in_specs=in_specs,
        out_specs=tuple(out_specs),
        out_shape=tuple(out_shape),
        scratch_shapes=scratch,
        compiler_params=pltpu.CompilerParams(
            dimension_semantics=("arbitrary",),
            vmem_limit_bytes=VMEM_LIMIT),
        name="trunk_sample" if has_state else "trunk_prompt",
    )(*args)


def kernel(x_prompt, x_sample, state_conv, cache_k, cache_v, a_pre_g, a_w_in, a_b_in, a_w_dw, a_b_dw, a_ln_g, a_ln_b, a_w_out, a_b_out, a_post_g, kv_g, w_kv, b_pre_g, b_w_in, b_sinks, b_w_out, b_post_g):
    row = lambda v: v.reshape(1, -1).astype(f32)
    w_dw = jnp.broadcast_to(a_w_dw[0].astype(f32).reshape(CONV_WIDTH, NBLK, 1, LANES),
                            (CONV_WIDTH, NBLK, SUB, LANES)).reshape(CONV_WIDTH * NBLK * SUB, LANES)
    b_dw = jnp.broadcast_to(a_b_dw[0].astype(f32).reshape(NBLK, 1, LANES),
                            (NBLK, SUB, LANES)).reshape(NBLK * SUB, LANES)
    col = lambda v: v.astype(f32).reshape(-1, 1)
    qscale = jnp.concatenate([jnp.full((1, D), HEAD_DIM ** -0.5 * LOG2E, f32),
                              jnp.ones((1, D), f32)], axis=1)
    weights = ((col(a_pre_g[0]) * a_w_in[0]).astype(bf16), row(a_b_in[0]),
               w_dw, b_dw, row(a_ln_g[0]), row(a_ln_b[0]),
               a_w_out[0].astype(bf16), row(a_b_out[0]), row(a_post_g[0]),
               (col(kv_g) * w_kv).astype(bf16),
               (col(b_pre_g[0]) * b_w_in[0] * qscale).astype(bf16),
               b_w_out[0].astype(bf16), row(b_post_g[0]))
    sinks = b_sinks[0].astype(f32)

    B, T, _ = x_prompt.shape
    yp, tail_p, kv_p = _trunk(x_prompt.reshape(B * T, D), None, None, None,
                              weights, sinks, seg_rows=TM, first_bias=0)
    y_prompt = yp.reshape(B, T, D)
    tp = tail_p[-NBLK * HIST * SUB:].reshape(NBLK, HIST, SUB, LANES)[:, :, SUB - 1, :]
    conv_p = tp.transpose(1, 0, 2).reshape(1, 1, HIST, D)
    k_p = kv_p[-WINDOW:, 0:LANES].reshape(B, WINDOW, N_KV, HEAD_DIM)
    v_p = kv_p[-WINDOW:, LANES:].reshape(B, WINDOW, N_KV, HEAD_DIM)

    SB, ST, _ = x_sample.shape
    hist = state_conv[0].astype(f32).reshape(SB, HIST, NBLK, LANES).transpose(2, 1, 0, 3)
    hist = hist.reshape(NBLK * HIST * SUB, LANES)
    kpre = cache_k.astype(f32).reshape(SB, WINDOW, LANES)
    vpre = cache_v.astype(f32).reshape(SB, WINDOW, LANES)
    ys, tail_s, _, ks, vs = _trunk(x_sample.reshape(SB * ST, D), hist, kpre, vpre,
                                   weights, sinks, seg_rows=ST, first_bias=2)
    y_sample = ys.reshape(SB, ST, D)
    conv_s = tail_s.reshape(NBLK, HIST, SB, LANES).transpose(2, 1, 0, 3).reshape(1, SB, HIST, D)
    k_s = ks.reshape(SB, WINDOW, N_KV, HEAD_DIM)
    v_s = vs.reshape(SB, WINDOW, N_KV, HEAD_DIM)
    return (y_prompt, y_sample, conv_p, conv_s, k_p, v_p, k_s, v_s)
```
